```python
import math
import jax, jax.numpy as jnp
from jax import lax
import numpy as np

D_MODEL = 1024
BATCH = 32
SEQ = 256
DEPTH = 4
DEC_BATCH = 2
DEC_SEQ = 2048
PAST_LEN = 512

GRID_W = 64
N_HEADS = 8
HEAD_DIM = 64
W_ATT = N_HEADS * HEAD_DIM
NA_ROWS = 8
NA_COLS = 16
NA_SPAN = 2 * NA_COLS
ROPE_BASE = 10000.0
Q_BLOCK = 128
W_CONV = 512
CONV_K = 31
W_POOL = 512
POOL_WINDOWS = (2, 4, 8, 16)
POOL_GROUP = W_POOL // len(POOL_WINDOWS)
W_SGU = 512
SGU_GROUPS = 4
SGU_GROUP_W = W_SGU // SGU_GROUPS
SGU_CHUNK = 128
N_BRANCH = 4
BRANCH_W = 512
N_EXPERTS = 16
EC_CAPACITY = 2
D_EXPERT = 1024
EPS = 1e-6
NEG_INF = -1e30

IN_SIZES = (W_ATT, W_ATT, W_ATT, 2 * W_CONV, W_POOL, W_SGU, W_SGU, N_BRANCH * D_MODEL)
D_IN = sum(IN_SIZES)
IN_SPLITS = tuple(int(s) for s in np.cumsum(IN_SIZES)[:-1])

kernel_name = "hybrid_diffusion_parallel_mixer_step"


def rmsnorm(x, g):
    xf = x.astype(jnp.float32)
    y = xf * lax.rsqrt(jnp.mean(xf * xf, axis=-1, keepdims=True) + EPS)
    return (y * g.astype(jnp.float32)).astype(x.dtype)


def layernorm(x, g, b):
    xf = x.astype(jnp.float32)
    mu = jnp.mean(xf, axis=-1, keepdims=True)
    var = jnp.mean(jnp.square(xf - mu), axis=-1, keepdims=True)
    y = (xf - mu) * lax.rsqrt(var + EPS)
    return (y * g.astype(jnp.float32) + b.astype(jnp.float32)).astype(x.dtype)


def modulate(h, shift, scale):
    return h * (1 + scale[:, None, :]) + shift[:, None, :]


def adaln(cond, w, b):
    return jax.nn.silu(cond) @ w + b


def axial_rope(x):
    L = x.shape[1]
    t = np.arange(L)
    half = HEAD_DIM // 2
    nf = half // 2
    inv = (1.0 / (ROPE_BASE ** (np.arange(nf) / nf))).astype(np.float32)

    def rot(xa, pos):
        ang = (pos[:, None].astype(np.float32) * inv[None, :]).astype(np.float32)
        cos = jnp.asarray(np.cos(ang))[None, :, None, :]
        sin = jnp.asarray(np.sin(ang))[None, :, None, :]
        x1, x2 = xa[..., :nf], xa[..., nf:]
        return jnp.concatenate([x1 * cos - x2 * sin, x1 * sin + x2 * cos], axis=-1)

    xf = x.astype(jnp.float32)
    out = jnp.concatenate([rot(xf[..., :half], t // GRID_W), rot(xf[..., half:], t % GRID_W)], axis=-1)
    return out.astype(x.dtype)


def dense_context_attention(q, ck, cv):
    B, Lc, H, hd = q.shape
    nb = Lc // Q_BLOCK
    qb = q.reshape(B, nb, Q_BLOCK, H, hd).transpose(1, 0, 2, 3, 4)

    def one(qblk):
        s = jnp.einsum('bqhd,bhkd->bhqk', qblk, ck).astype(jnp.float32)
        p = jax.nn.softmax(s, axis=-1).astype(cv.dtype)
        return jnp.einsum('bhqk,bhkd->bqhd', p, cv)

    o = lax.map(one, qb)
    return o.transpose(1, 0, 2, 3, 4).reshape(B, Lc, H, hd)


def neighbourhood_attention(q_rot, q_plain, k_rot, v, ck, cv, rpb):
    B, L, H, hd = q_rot.shape
    rows = L // GRID_W
    kh = min(NA_ROWS, rows)
    ncb = GRID_W // NA_COLS
    r = np.arange(rows)
    row_idx = np.clip(r - kh // 2, 0, rows - kh)[:, None] + np.arange(kh)[None, :]
    d_row = row_idx - r[:, None] + (NA_ROWS - 1)
    cb = np.arange(ncb)
    span_idx = np.clip(cb * NA_COLS - NA_COLS // 2, 0, GRID_W - NA_SPAN)[:, None] + np.arange(NA_SPAN)[None, :]
    qcol = np.arange(GRID_W).reshape(ncb, NA_COLS)
    col_start = np.clip(qcol - NA_COLS // 2, 0, GRID_W - NA_COLS)
    keycol = span_idx[:, None, :]
    col_valid = (keycol >= col_start[..., None]) & (keycol < col_start[..., None] + NA_COLS)
    d_col = np.clip(keycol - qcol[..., None], -(NA_COLS - 1), NA_COLS - 1) + (NA_COLS - 1)
    bias = rpb.astype(jnp.float32)[:, d_row[:, None, None, :, None], d_col[None, :, :, None, :]]
    bias = jnp.where(jnp.asarray(col_valid)[None, None, :, :, None, :], bias, NEG_INF)
    bias = bias.transpose(1, 2, 0, 3, 4, 5)

    kg = k_rot.reshape(B, rows, GRID_W, H, hd)
    vg = v.reshape(B, rows, GRID_W, H, hd)
    to_rows = lambda a: a.reshape(B, rows, ncb, NA_COLS, H, hd).transpose(1, 0, 2, 3, 4, 5)
    n_loc = kh * NA_SPAN

    def one_row(args):
        qr, qp, ridx, brow = args
        kb = kg[:, ridx][:, :, span_idx]
        vb = vg[:, ridx][:, :, span_idx]
        s_loc = jnp.einsum('bnqhd,bknshd->bnhqks', qr, kb).astype(jnp.float32) + brow[None]
        s_ctx = jnp.einsum('bnqhd,bhcd->bnhqc', qp, ck).astype(jnp.float32)
        s = jnp.concatenate([s_loc.reshape(B, ncb, H, NA_COLS, n_loc), s_ctx], axis=-1)
        p = jax.nn.softmax(s, axis=-1).astype(v.dtype)
        p_loc = p[..., :n_loc].reshape(B, ncb, H, NA_COLS, kh, NA_SPAN)
        p_ctx = p[..., n_loc:]
        return (jnp.einsum('bnhqks,bknshd->bnqhd', p_loc, vb)
                + jnp.einsum('bnhqc,bhcd->bnqhd', p_ctx, cv))

    o = lax.map(one_row, (to_rows(q_rot), to_rows(q_plain), jnp.asarray(row_idx, jnp.int32), bias))
    return o.transpose(1, 0, 2, 3, 4, 5).reshape(B, L, H, hd)


def context_attention_branch(q, k, v):
    B, L = q.shape[:2]
    ck, cv = k.transpose(0, 2, 1, 3), v.transpose(0, 2, 1, 3)
    o = dense_context_attention(q * (HEAD_DIM ** -0.5), ck, cv)
    return o.reshape(B, L, W_ATT), (ck, cv)


def latent_attention_branch(q, k, v, ck, cv, rpb):
    B, L = q.shape[:2]
    scale = HEAD_DIM ** -0.5
    o = neighbourhood_attention(axial_rope(q) * scale, q * scale, axial_rope(k), v, ck, cv, rpb)
    return o.reshape(B, L, W_ATT), None


def conformer_conv(z, conv_w, conv_b, ln_g, ln_b):
    a, g = jnp.split(z, 2, axis=-1)
    y = a * jax.nn.sigmoid(g)
    y = lax.conv_general_dilated(y, conv_w[:, None, :], window_strides=(1,),
                                 padding=[(CONV_K // 2, CONV_K // 2)],
                                 dimension_numbers=('NWC', 'WIO', 'NWC'),
                                 feature_group_count=W_CONV) + conv_b
    return jax.nn.silu(layernorm(y, ln_g, ln_b))


def multiscale_pool(z, pool_w, pool_scale):
    B, L, _ = z.shape
    ng = len(POOL_WINDOWS)
    zg = z.reshape(B, L, ng, POOL_GROUP)
    zf = zg.astype(jnp.float32)
    csum = jnp.concatenate([jnp.zeros((B, 1, ng, POOL_GROUP), jnp.float32), jnp.cumsum(zf, axis=1)], axis=1)
    t = np.arange(L)
    means = []
    for gi, w in enumerate(POOL_WINDOWS):
        lo = np.clip(t - w // 2, 0, L - 1)
        hi = np.clip(t + w - w // 2 - 1, 0, L - 1)
        cnt = jnp.asarray((hi - lo + 1).astype(np.float32))
        means.append((csum[:, hi + 1, gi] - csum[:, lo, gi]) / cnt[None, :, None])
    d = (jnp.stack(means, axis=2) - zf).astype(z.dtype)
    y = jnp.einsum('blgc,gce->blge', d, pool_w).reshape(B, L, W_POOL)
    return y * pool_scale


def spatial_gating(u, v, ln_g, ln_b, sgu_w, sgu_b):
    B, L, _ = u.shape
    nc = L // SGU_CHUNK
    vg = layernorm(v, ln_g, ln_b).reshape(B, nc, SGU_CHUNK, SGU_GROUPS, SGU_GROUP_W)
    mixed = jnp.einsum('gpq,bnqgc->bnpgc', sgu_w, vg) + sgu_b.T[None, None, :, :, None]
    return u * mixed.reshape(B, L, W_SGU)


def expert_choice_ffn(h, w_router, w1, w3, w2):
    B, L, D = h.shape
    cap = EC_CAPACITY * L // N_EXPERTS
    aff = jax.nn.softmax((h @ w_router).astype(jnp.float32), axis=-1)
    gate, idx = lax.top_k(aff.transpose(0, 2, 1), cap)
    xs = jax.vmap(lambda hb, ib: hb[ib])(h, idx)
    a = jnp.einsum('becd,edf->becf', xs, w1)
    b = jnp.einsum('becd,edf->becf', xs, w3)
    y = jnp.einsum('becf,efd->becd', jax.nn.silu(a) * b, w2) * gate[..., None].astype(h.dtype)
    return jax.vmap(lambda yb, ib: jnp.zeros((L, D), yb.dtype).at[ib.reshape(-1)].add(yb.reshape(-1, D)))(y, idx)


def token_mixers(h, attn_fn, lp):
    B, L, _ = h.shape
    proj = h @ lp['w_in']
    q, k, v, z_conv, z_pool, z_u, z_v, z_gate = jnp.split(proj, IN_SPLITS, axis=-1)
    shp = (B, L, N_HEADS, HEAD_DIM)
    o_att, kv = attn_fn(q.reshape(shp), k.reshape(shp), v.reshape(shp))
    o_conv = conformer_conv(z_conv, lp['conv_w'], lp['conv_b'], lp['conv_ln_g'], lp['conv_ln_b'])
    o_pool = multiscale_pool(z_pool, lp['pool_w'], lp['pool_scale'])
    o_sgu = spatial_gating(jax.nn.gelu(z_u), jax.nn.gelu(z_v), lp['sgu_ln_g'], lp['sgu_ln_b'], lp['sgu_w'], lp['sgu_b'])
    branches = jnp.stack([o_att, o_conv, o_pool, o_sgu], axis=2)
    proj_b = jnp.einsum('blnc,ncd->blnd', branches, lp['w_branch'])
    gates = jax.nn.sigmoid(z_gate.reshape(B, L, N_BRANCH, D_MODEL))
    merged = jnp.sum(gates * proj_b, axis=2)
    return merged @ lp['w_out'], kv


def run_layer(x, mod, attn_fn, lp):
    shift1, scale1, gate1, shift2, scale2, gate2 = jnp.split(mod, 6, axis=-1)
    h = modulate(rmsnorm(x, lp['norm1_g']), shift1, scale1)
    mix, kv = token_mixers(h, attn_fn, lp)
    x = x + gate1[:, None, :] * mix
    h2 = modulate(rmsnorm(x, lp['norm2_g']), shift2, scale2)
    x = x + gate2[:, None, :] * expert_choice_ffn(h2, lp['w_router'], lp['w1'], lp['w3'], lp['w2'])
    return x, kv


def setup_inputs(seed: int = 0) -> dict:
    key = jax.random.key(seed)
    ks = iter(jax.random.split(key, 40))
    nrm = lambda shape, s: jax.random.normal(next(ks), shape, jnp.float32) * s
    D = D_MODEL
    return {
        'x_prompt': nrm((BATCH, SEQ, D), 1.0),
        'x_sample': nrm((DEC_BATCH, DEC_SEQ, D), 1.0),
        'cache_k': nrm((DEC_BATCH, DEPTH, N_HEADS, PAST_LEN, HEAD_DIM), 1.0),
        'cache_v': nrm((DEC_BATCH, DEPTH, N_HEADS, PAST_LEN, HEAD_DIM), 1.0),
        'c': nrm((DEC_BATCH, D), 1.0),
        'c_ctx': nrm((D,), 1.0),
        'w_ada': nrm((DEPTH, D, 6 * D), 0.5 * D ** -0.5),
        'b_ada': nrm((DEPTH, 6 * D), 0.02),
        'norm1_g': 1.0 + nrm((DEPTH, D), 0.02),
        'w_in': nrm((DEPTH, D, D_IN), D ** -0.5),
        'rpb': nrm((DEPTH, N_HEADS, 2 * NA_ROWS - 1, 2 * NA_COLS - 1), 0.1),
        'conv_w': nrm((DEPTH, CONV_K, W_CONV), CONV_K ** -0.5),
        'conv_b': nrm((DEPTH, W_CONV), 0.02),
        'conv_ln_g': 1.0 + nrm((DEPTH, W_CONV), 0.02),
        'conv_ln_b': nrm((DEPTH, W_CONV), 0.02),
        'pool_w': nrm((DEPTH, len(POOL_WINDOWS), POOL_GROUP, POOL_GROUP), POOL_GROUP ** -0.5),
        'pool_scale': 1.0 + nrm((DEPTH, W_POOL), 0.02),
        'sgu_ln_g': 1.0 + nrm((DEPTH, W_SGU), 0.02),
        'sgu_ln_b': nrm((DEPTH, W_SGU), 0.02),
        'sgu_w': nrm((DEPTH, SGU_GROUPS, SGU_CHUNK, SGU_CHUNK), SGU_CHUNK ** -0.5),
        'sgu_b': 1.0 + nrm((DEPTH, SGU_GROUPS, SGU_CHUNK), 0.02),
        'w_branch': nrm((DEPTH, N_BRANCH, BRANCH_W, D), BRANCH_W ** -0.5),
        'w_out': nrm((DEPTH, D, D), D ** -0.5),
        'norm2_g': 1.0 + nrm((DEPTH, D), 0.02),
        'w_router': nrm((DEPTH, D, N_EXPERTS), D ** -0.5),
        'w1': nrm((DEPTH, N_EXPERTS, D, D_EXPERT), D ** -0.5),
        'w3': nrm((DEPTH, N_EXPERTS, D, D_EXPERT), D ** -0.5),
        'w2': nrm((DEPTH, N_EXPERTS, D_EXPERT, D), D_EXPERT ** -0.5),
        'final_g': 1.0 + nrm((D,), 0.02),
    }


def reference(x_prompt, x_sample, cache_k, cache_v, c, c_ctx, w_ada, b_ada, norm1_g, w_in, rpb,
              conv_w, conv_b, conv_ln_g, conv_ln_b, pool_w, pool_scale, sgu_ln_g, sgu_ln_b, sgu_w, sgu_b,
              w_branch, w_out, norm2_g, w_router, w1, w3, w2, final_g):
    xp, xs = x_prompt, x_sample
    new_k, new_v = [], []
    for l in range(DEPTH):
        lp = {
            'norm1_g': norm1_g[l], 'w_in': w_in[l],
            'conv_w': conv_w[l], 'conv_b': conv_b[l], 'conv_ln_g': conv_ln_g[l], 'conv_ln_b': conv_ln_b[l],
            'pool_w': pool_w[l], 'pool_scale': pool_scale[l],
            'sgu_ln_g': sgu_ln_g[l], 'sgu_ln_b': sgu_ln_b[l], 'sgu_w': sgu_w[l], 'sgu_b': sgu_b[l],
            'w_branch': w_branch[l], 'w_out': w_out[l], 'norm2_g': norm2_g[l],
            'w_router': w_router[l], 'w1': w1[l], 'w3': w3[l], 'w2': w2[l],
        }
        mod_ctx = adaln(c_ctx[None, :], w_ada[l], b_ada[l])
        xp, (k_l, v_l) = run_layer(xp, mod_ctx, context_attention_branch, lp)
        new_k.append(k_l)
        new_v.append(v_l)
        mod_lat = adaln(c, w_ada[l], b_ada[l])
        ck, cv, rpb_l = cache_k[:, l], cache_v[:, l], rpb[l]
        attn_lat = lambda q, k, v, ck=ck, cv=cv, rpb_l=rpb_l: latent_attention_branch(q, k, v, ck, cv, rpb_l)
        xs, _ = run_layer(xs, mod_lat, attn_lat, lp)
    y_prompt = rmsnorm(xp, final_g)
    y_sample = rmsnorm(xs, final_g)
    new_cache_k = jnp.stack(new_k, axis=1)
    new_cache_v = jnp.stack(new_v, axis=1)
    return (y_prompt, y_sample, new_cache_k, new_cache_v)
```

```python
import functools

import numpy as np
import jax
import jax.numpy as jnp
from jax import lax
from jax.experimental import pallas as pl
from jax.experimental.pallas import tpu as pltpu

F32 = jnp.float32
BF16 = jnp.bfloat16

D_MODEL = 1024
BATCH = 32
SEQ = 256
DEPTH = 4
DEC_BATCH = 2
DEC_SEQ = 2048
PAST_LEN = 512
GRID_W = 64
GRID_ROWS = DEC_SEQ // GRID_W
N_HEADS = 8
HEAD_DIM = 64
W_ATT = N_HEADS * HEAD_DIM
NA_ROWS = 8
NA_COLS = 16
ROPE_BASE = 10000.0
W_CONV = 512
CONV_K = 31
W_POOL = 512
POOL_WINDOWS = (2, 4, 8, 16)
POOL_GROUP = 128
W_SGU = 512
SGU_GROUPS = 4
SGU_CHUNK = 128
N_BRANCH = 4
BRANCH_W = 512
N_EXPERTS = 16
EC_CAPACITY = 2
EPS = 1e-6
NEG_INF = -1e30

N_CTX = BATCH * SEQ
N_LAT = DEC_BATCH * DEC_SEQ
N_TOK = N_CTX + N_LAT
D_PROJ = 4096
COL_BLK = 512
CAP_CTX = EC_CAPACITY * SEQ // N_EXPERTS
CAP_LAT = EC_CAPACITY * DEC_SEQ // N_EXPERTS
N_ROW_OFF = 2 * NA_ROWS - 2
HALO = 16
CHUNK = 256
LANES = 128
SUBLANES = 8
VMEM_LIMIT_CAP = 56 * 1024 * 1024

HIGHEST = lax.Precision.HIGHEST


def _cparams(n_grid, vmem_mb):
    return pltpu.CompilerParams(
        dimension_semantics=("arbitrary",) * n_grid,
        vmem_limit_bytes=min(vmem_mb * 1024 * 1024, VMEM_LIMIT_CAP))


def _dot(a, b):
    return jnp.dot(a, b, preferred_element_type=F32)


def _dot_nt(a, b):
    return lax.dot_general(a, b, (((1,), (1,)), ((), ())), preferred_element_type=F32)


def _dot_tn(a, b):
    return lax.dot_general(a, b, (((0,), (0,)), ((), ())), preferred_element_type=F32)


def _sigmoid(x):
    return 0.5 * jnp.tanh(0.5 * x) + 0.5


def _segment_of_row(row0):
    return jnp.where(row0 < N_CTX, 0, 1 + (row0 - N_CTX) // DEC_SEQ)


def _mod_row(mod_ref, seg, k):
    return mod_ref[pl.ds(seg, 1), k * D_MODEL:(k + 1) * D_MODEL]


def _norm_modulate(x, g, shift, scale):
    y = x * lax.rsqrt(jnp.mean(x * x, axis=-1, keepdims=True) + EPS) * g
    return y * (1.0 + scale) + shift


def _layernorm(x, g, b):
    mu = jnp.mean(x, axis=-1, keepdims=True)
    xc = x - mu
    var = jnp.mean(xc * xc, axis=-1, keepdims=True)
    return xc * lax.rsqrt(var + EPS) * g + b


def _adaln_body(cond_ref, w_ref, b_ref, o_ref):
    c = cond_ref[...]
    a = (c * _sigmoid(c)).astype(BF16)
    o_ref[...] = _dot(a, w_ref[...].astype(BF16)) + b_ref[...]


def _adaln(cond8, w_ada, b_ada):
    tn = 1536
    n6 = 6 * D_MODEL
    return pl.pallas_call(
        _adaln_body,
        grid=(DEPTH, n6 // tn),
        in_specs=[
            pl.BlockSpec((8, D_MODEL), lambda l, j: (0, 0)),
            pl.BlockSpec((None, D_MODEL, tn), lambda l, j: (l, 0, j)),
            pl.BlockSpec((None, 1, tn), lambda l, j: (l, 0, j)),
        ],
        out_specs=pl.BlockSpec((None, 8, tn), lambda l, j: (l, 0, j)),
        out_shape=jax.ShapeDtypeStruct((DEPTH, 8, n6), F32),
        compiler_params=_cparams(2, 32),
        name="adaln",
    )(cond8, w_ada, b_ada.reshape(DEPTH, 1, n6))


def _rpb_onehots():
    cq = np.arange(GRID_W)[:, None]
    kc = np.arange(GRID_W)[None, :]
    start = np.clip(cq - NA_COLS // 2, 0, GRID_W - NA_COLS)
    valid = (kc >= start) & (kc < start + NA_COLS)
    d_col = np.clip(kc - cq, -(NA_COLS - 1), NA_COLS - 1) + (NA_COLS - 1)
    left = np.zeros((32, GRID_W, 2 * GRID_W), np.float32)
    right = np.zeros((32, GRID_W, 2 * GRID_W), np.float32)
    mask = np.zeros((GRID_W, 2 * GRID_W), np.float32)
    for j in range(2 * NA_COLS - 1):
        hit = (d_col == j) & valid
        left[j, :, :GRID_W] = hit
        right[j, :, GRID_W:] = hit
    mask[:, :GRID_W] = np.where(valid, 0.0, NEG_INF)
    mask[:, GRID_W:] = np.where(valid, 0.0, NEG_INF)
    n = GRID_W * 2 * GRID_W
    return left.reshape(32, n), right.reshape(32, n), mask.reshape(1, n)


def _rpb_body(a_ref, b_ref, left_ref, right_ref, mask_ref, o_ref):
    t = jnp.dot(a_ref[...], left_ref[...], precision=HIGHEST, preferred_element_type=F32)
    t = t + jnp.dot(b_ref[...], right_ref[...], precision=HIGHEST, preferred_element_type=F32)
    o_ref[...] = t + mask_ref[...]


def _rpb_table(rpb):
    left, right, mask = _rpb_onehots()
    n_rows = N_HEADS * N_ROW_OFF
    pad = ((0, 0), (0, 0), (0, 0), (0, 1))
    rpb_p = jnp.pad(rpb, pad)
    a = rpb_p[:, :, :N_ROW_OFF].reshape(DEPTH, n_rows, 32)
    b = rpb_p[:, :, 1:N_ROW_OFF + 1].reshape(DEPTH, n_rows, 32)
    n = left.shape[1]
    out = pl.pallas_call(
        _rpb_body,
        grid=(DEPTH,),
        in_specs=[
            pl.BlockSpec((None, n_rows, 32), lambda l: (l, 0, 0)),
            pl.BlockSpec((None, n_rows, 32), lambda l: (l, 0, 0)),
            pl.BlockSpec((32, n), lambda l: (0, 0)),
            pl.BlockSpec((32, n), lambda l: (0, 0)),
            pl.BlockSpec((1, n), lambda l: (0, 0)),
        ],
        out_specs=pl.BlockSpec((None, n_rows, n), lambda l: (l, 0, 0)),
        out_shape=jax.ShapeDtypeStruct((DEPTH, n_rows, n), F32),
        compiler_params=_cparams(1, 32),
        name="rpb_table",
    )(a, b, jnp.asarray(left), jnp.asarray(right), jnp.asarray(mask))
    return out.reshape(DEPTH, N_HEADS, N_ROW_OFF, GRID_W, 2 * GRID_W)


def _proj_body(x_ref, g_ref, mod_ref, w_ref, o_ref, h_scr, *, tm):
    i = pl.program_id(0)

    @pl.when(pl.program_id(1) == 0)
    def _():
        seg = _segment_of_row(i * tm)
        h = _norm_modulate(x_ref[...], g_ref[...], _mod_row(mod_ref, seg, 0), _mod_row(mod_ref, seg, 1))
        h_scr[...] = h.astype(BF16)

    o_ref[...] = _dot(h_scr[...], w_ref[...]).astype(BF16)


def _proj(x, norm_g, mod, w_in_bf, l):
    tm, tn = 1024, 1024
    return pl.pallas_call(
        functools.partial(_proj_body, tm=tm),
        grid=(N_TOK // tm, D_PROJ // tn),
        in_specs=[
            pl.BlockSpec((tm, D_MODEL), lambda i, j: (i, 0)),
            pl.BlockSpec((None, 1, D_MODEL), lambda i, j: (l, 0, 0)),
            pl.BlockSpec((None, 8, 6 * D_MODEL), lambda i, j: (l, 0, 0)),
            pl.BlockSpec((None, D_MODEL, tn), lambda i, j: (l, 0, j)),
        ],
        out_specs=pl.BlockSpec((tm, tn), lambda i, j: (i, j)),
        out_shape=jax.ShapeDtypeStruct((N_TOK, D_PROJ), BF16),
        scratch_shapes=[pltpu.VMEM((tm, D_MODEL), BF16)],
        compiler_params=_cparams(2, 40),
        name="proj",
    )(x, norm_g, mod, w_in_bf)


def _attn_ctx_body(q_ref, k_ref, v_ref, ck_in, cv_in, o_ref, ck_ref, cv_ref):
    del ck_in, cv_in
    q = q_ref[...]
    k = k_ref[...]
    v = v_ref[...]
    outs = []
    for h in range(N_HEADS):
        sl = slice(h * HEAD_DIM, (h + 1) * HEAD_DIM)
        kh = k[:, sl]
        vh = v[:, sl]
        s = _dot_nt(q[:, sl], kh) * (HEAD_DIM ** -0.5)
        p = jnp.exp(s - jnp.max(s, axis=-1, keepdims=True))
        denom = jnp.sum(p, axis=-1, keepdims=True)
        outs.append(_dot(p.astype(BF16), vh) / denom)
        ck_ref[h] = kh.astype(F32)
        cv_ref[h] = vh.astype(F32)
    o_ref[...] = jnp.concatenate(outs, axis=-1).astype(BF16)


def _attn_ctx(proj, cache_k, cache_v, l):
    cache_spec = pl.BlockSpec((None, None, N_HEADS, SEQ, HEAD_DIM), lambda b: (b, l, 0, 0, 0))
    cache_shape = jax.ShapeDtypeStruct((BATCH, DEPTH, N_HEADS, SEQ, HEAD_DIM), F32)
    return pl.pallas_call(
        _attn_ctx_body,
        grid=(BATCH,),
        in_specs=[
            pl.BlockSpec((SEQ, COL_BLK), lambda b: (b, 0)),
            pl.BlockSpec((SEQ, COL_BLK), lambda b: (b, 1)),
            pl.BlockSpec((SEQ, COL_BLK), lambda b: (b, 2)),
            pl.BlockSpec(memory_space=pl.ANY),
            pl.BlockSpec(memory_space=pl.ANY),
        ],
        out_specs=[pl.BlockSpec((SEQ, W_ATT), lambda b: (b, 0)), cache_spec, cache_spec],
        out_shape=[jax.ShapeDtypeStruct((N_CTX, W_ATT), BF16), cache_shape, cache_shape],
        input_output_aliases={3: 1, 4: 2},
        compiler_params=_cparams(1, 32),
        name="attn_ctx",
    )(proj, proj, proj, cache_k, cache_v)


def _rope_tables():
    t = np.arange(DEC_SEQ)
    half = HEAD_DIM // 2
    nf = half // 2
    inv = (1.0 / (ROPE_BASE ** (np.arange(nf) / nf))).astype(np.float32)
    cos = np.zeros((DEC_SEQ, HEAD_DIM), np.float32)
    sin = np.zeros((DEC_SEQ, HEAD_DIM), np.float32)
    for blk, pos in enumerate((t // GRID_W, t % GRID_W)):
        ang = (pos[:, None].astype(np.float32) * inv[None, :]).astype(np.float32)
        c, s = np.cos(ang), np.sin(ang)
        cos[:, blk * half:(blk + 1) * half] = np.concatenate([c, c], axis=1)
        sin[:, blk * half:(blk + 1) * half] = np.concatenate([-s, s], axis=1)
    return np.tile(cos, (1, 2)), np.tile(sin, (1, 2))


def _rope(x, cos, sin):
    cos = jnp.concatenate([cos] * (W_ATT // LANES), axis=1)
    sin = jnp.concatenate([sin] * (W_ATT // LANES), axis=1)
    lane = lax.broadcasted_iota(jnp.int32, x.shape, 1)
    nf = HEAD_DIM // 4
    partner = jnp.where(lane % (2 * nf) < nf, pltpu.roll(x, W_ATT - nf, 1), pltpu.roll(x, nf, 1))
    return x * cos + partner * sin


def _attn_lat_body(q_ref, k_ref, v_ref, ck_ref, cv_ref, tab_ref, cos_ref, sin_ref, o_ref,
                   krot_scr, ck_scr, cv_scr):
    r = pl.program_id(1)
    rope_rows = 256

    @pl.when(r == 0)
    def _():
        def chunk(c, carry):
            rows = pl.ds(pl.multiple_of(c * rope_rows, rope_rows), rope_rows)
            kr = _rope(k_ref[rows, :].astype(F32), cos_ref[rows, :], sin_ref[rows, :])
            krot_scr[rows, :] = kr.astype(BF16)
            return carry
        lax.fori_loop(0, DEC_SEQ // rope_rows, chunk, 0)
        ck_scr[...] = ck_ref[...].astype(BF16)
        cv_scr[...] = cv_ref[...].astype(BF16)

    kh = min(NA_ROWS, GRID_ROWS)
    row_start = jnp.clip(r - kh // 2, 0, GRID_ROWS - kh)
    off = row_start - r + (NA_ROWS - 1)
    n_loc = kh * GRID_W
    scale = HEAD_DIM ** -0.5

    qrows = pl.ds(pl.multiple_of(r * GRID_W, GRID_W), GRID_W)
    q = q_ref[...].astype(F32)
    q_rot = (_rope(q, cos_ref[qrows, :], sin_ref[qrows, :]) * scale).astype(BF16)
    q_plain = (q * scale).astype(BF16)
    win = pl.ds(pl.multiple_of(row_start * GRID_W, GRID_W), n_loc)
    k_win = krot_scr[win, :]
    v_win = v_ref[win, :]

    outs = []
    for h in range(N_HEADS):
        sl = slice(h * HEAD_DIM, (h + 1) * HEAD_DIM)
        bias = jnp.concatenate([tab_ref[h, off + 2 * j] for j in range(kh // 2)], axis=1)
        s_loc = _dot_nt(q_rot[:, sl], k_win[:, sl]) + bias
        s_ctx = _dot_nt(q_plain[:, sl], ck_scr[h])
        m = jnp.maximum(jnp.max(s_loc, axis=-1, keepdims=True), jnp.max(s_ctx, axis=-1, keepdims=True))
        p_loc = jnp.exp(s_loc - m)
        p_ctx = jnp.exp(s_ctx - m)
        denom = jnp.sum(p_loc, axis=-1, keepdims=True) + jnp.sum(p_ctx, axis=-1, keepdims=True)
        o = _dot(p_loc.astype(BF16), v_win[:, sl]) + _dot(p_ctx.astype(BF16), cv_scr[h])
        outs.append(o / denom)
    o_ref[...] = jnp.concatenate(outs, axis=-1).astype(BF16)


def _attn_lat(proj, cache_k, cache_v, table, cos, sin, l):
    q_blk0 = N_CTX // GRID_W
    kv_blk0 = N_CTX // DEC_SEQ
    cache_spec = pl.BlockSpec((None, None, N_HEADS, PAST_LEN, HEAD_DIM), lambda b, r: (b, l, 0, 0, 0))
    return pl.pallas_call(
        _attn_lat_body,
        grid=(DEC_BATCH, GRID_ROWS),
        in_specs=[
            pl.BlockSpec((GRID_W, COL_BLK), lambda b, r: (q_blk0 + b * GRID_ROWS + r, 0)),
            pl.BlockSpec((DEC_SEQ, COL_BLK), lambda b, r: (kv_blk0 + b, 1)),
            pl.BlockSpec((DEC_SEQ, COL_BLK), lambda b, r: (kv_blk0 + b, 2)),
            cache_spec, cache_spec,
            pl.BlockSpec((None, N_HEADS, N_ROW_OFF, GRID_W, 2 * GRID_W), lambda b, r: (l, 0, 0, 0, 0)),
            pl.BlockSpec((DEC_SEQ, LANES), lambda b, r: (0, 0)),
            pl.BlockSpec((DEC_SEQ, LANES), lambda b, r: (0, 0)),
        ],
        out_specs=pl.BlockSpec((GRID_W, W_ATT), lambda b, r: (b * GRID_ROWS + r, 0)),
        out_shape=jax.ShapeDtypeStruct((N_LAT, W_ATT), BF16),
        scratch_shapes=[
            pltpu.VMEM((DEC_SEQ, W_ATT), BF16),
            pltpu.VMEM((N_HEADS, PAST_LEN, HEAD_DIM), BF16),
            pltpu.VMEM((N_HEADS, PAST_LEN, HEAD_DIM), BF16),
        ],
        compiler_params=_cparams(2, 48),
        name="attn_lat",
    )(proj, proj, proj, cache_k, cache_v, table, cos, sin)


def _seqmix_body(a_ref, g_ref, p_ref, ap_ref, gp_ref, pp_ref, an_ref, gn_ref, pn_ref, u_ref, v_ref,
                 cw_ref, cb_ref, clg_ref, clb_ref, pw_ref, ps_ref, slg_ref, slb_ref, sw_ref, sbt_ref,
                 oc_ref, op_ref, os_ref, ypad, zpad, yshift):
    i = pl.program_id(0)
    n_ctx_chunks = N_CTX // CHUNK
    per_seq = DEC_SEQ // CHUNK
    j = (i - n_ctx_chunks) % per_seq
    is_lat = i >= n_ctx_chunks
    left_ok = jnp.logical_and(is_lat, j > 0)
    right_ok = jnp.logical_and(is_lat, j < per_seq - 1)
    seq_len = jnp.where(is_lat, DEC_SEQ, SEQ)
    chunk_off = jnp.where(is_lat, j * CHUNK, 0)

    def glu(a, g):
        return a.astype(F32) * _sigmoid(g.astype(F32))

    ypad[0:HALO, :] = jnp.where(left_ok, glu(ap_ref[...], gp_ref[...]), 0.0)
    ypad[HALO:HALO + CHUNK, :] = glu(a_ref[...], g_ref[...])
    ypad[HALO + CHUNK:, :] = jnp.where(right_ok, glu(an_ref[...], gn_ref[...]), 0.0)
    zpad[0:HALO, :] = jnp.where(left_ok, pp_ref[...].astype(F32), 0.0)
    zpad[HALO:HALO + CHUNK, :] = p_ref[...].astype(F32)
    zpad[HALO + CHUNK:, :] = jnp.where(right_ok, pn_ref[...].astype(F32), 0.0)

    for s in range(SUBLANES):
        yshift[s] = ypad[s:s + yshift.shape[1], :]
    rows = 64
    half_k = CONV_K // 2
    conv_rows = []
    for rb in range(CHUNK // rows):
        cols = []
        for cg in range(W_CONV // LANES):
            lanes = slice(cg * LANES, (cg + 1) * LANES)
            acc = jnp.zeros((rows, LANES), F32)
            for k in range(CONV_K):
                start = HALO + rb * rows + k - half_k
                s = start % SUBLANES
                acc = acc + cw_ref[k:k + 1, lanes] * yshift[s, start - s:start - s + rows, lanes]
            cols.append(acc)
        conv_rows.append(jnp.concatenate(cols, axis=1))
    y = jnp.concatenate(conv_rows, axis=0) + cb_ref[...]
    y = _layernorm(y, clg_ref[...], clb_ref[...])
    oc_ref[...] = (y * _sigmoid(y)).astype(BF16)

    t_seq = lax.broadcasted_iota(jnp.int32, (CHUNK, 1), 0) + chunk_off
    pooled = []
    for gi, w in enumerate(POOL_WINDOWS):
        lanes = slice(gi * POOL_GROUP, (gi + 1) * POOL_GROUP)
        lo, hi = -(w // 2), w - w // 2 - 1
        tot = jnp.zeros((CHUNK, POOL_GROUP), F32)
        for jj in range(lo, hi + 1):
            tot = tot + zpad[HALO + jj:HALO + jj + CHUNK, lanes]
        cnt = w - jnp.maximum(0, -lo - t_seq) - jnp.maximum(0, t_seq + hi - (seq_len - 1))
        d = tot / cnt.astype(F32) - zpad[HALO:HALO + CHUNK, lanes]
        pooled.append(_dot(d.astype(BF16), pw_ref[gi]))
    op_ref[...] = (jnp.concatenate(pooled, axis=1) * ps_ref[...]).astype(BF16)

    u = jax.nn.gelu(u_ref[...].astype(F32))
    vn = _layernorm(jax.nn.gelu(v_ref[...].astype(F32)), slg_ref[...], slb_ref[...]).astype(BF16)
    sg_rows = []
    for n in range(CHUNK // SGU_CHUNK):
        pos = slice(n * SGU_CHUNK, (n + 1) * SGU_CHUNK)
        cols = []
        for g in range(SGU_GROUPS):
            lanes = slice(g * LANES, (g + 1) * LANES)
            cols.append(_dot(sw_ref[g], vn[pos, lanes]) + sbt_ref[:, g:g + 1])
        sg_rows.append(jnp.concatenate(cols, axis=1))
    os_ref[...] = (u * jnp.concatenate(sg_rows, axis=0)).astype(BF16)


def _seqmix(proj, conv_w, conv_b, conv_ln_g, conv_ln_b, pool_w_bf, pool_scale, sgu_ln_g, sgu_ln_b,
            sgu_w_bf, sgu_bt, l):
    n_chunks = N_TOK // CHUNK
    halos_per_chunk = CHUNK // HALO
    n_halo = N_TOK // HALO

    def main(col):
        return pl.BlockSpec((CHUNK, COL_BLK), lambda i: (i, col))

    def prev(col):
        return pl.BlockSpec((HALO, COL_BLK), lambda i: (jnp.maximum(i * halos_per_chunk - 1, 0), col))

    def nxt(col):
        return pl.BlockSpec((HALO, COL_BLK), lambda i: (jnp.minimum((i + 1) * halos_per_chunk, n_halo - 1), col))

    def vec(n):
        return pl.BlockSpec((None, 1, n), lambda i: (l, 0, 0))

    out_spec = pl.BlockSpec((CHUNK, BRANCH_W), lambda i: (i, 0))
    out_shape = jax.ShapeDtypeStruct((N_TOK, BRANCH_W), BF16)
    return pl.pallas_call(
        _seqmix_body,
        grid=(n_chunks,),
        in_specs=[
            main(3), main(4), main(5), prev(3), prev(4), prev(5), nxt(3), nxt(4), nxt(5), main(6), main(7),
            pl.BlockSpec((None, CONV_K, W_CONV), lambda i: (l, 0, 0)),
            vec(W_CONV), vec(W_CONV), vec(W_CONV),
            pl.BlockSpec((None, len(POOL_WINDOWS), POOL_GROUP, POOL_GROUP), lambda i: (l, 0, 0, 0)),
            vec(W_POOL), vec(W_SGU), vec(W_SGU),
            pl.BlockSpec((None, SGU_GROUPS, SGU_CHUNK, SGU_CHUNK), lambda i: (l, 0, 0, 0)),
            pl.BlockSpec((None, SGU_CHUNK, SGU_GROUPS), lambda i: (l, 0, 0)),
        ],
        out_specs=[out_spec, out_spec, out_spec],
        out_shape=[out_shape, out_shape, out_shape],
        scratch_shapes=[
            pltpu.VMEM((CHUNK + 2 * HALO, W_CONV), F32),
            pltpu.VMEM((CHUNK + 2 * HALO, W_POOL), F32),
            pltpu.VMEM((SUBLANES, CHUNK + 2 * HALO - SUBLANES, W_CONV), F32),
        ],
        compiler_params=_cparams(1, 32),
        name="seqmix",
    )(*([proj] * 11), conv_w, conv_b, conv_ln_g, conv_ln_b, pool_w_bf, pool_scale, sgu_ln_g, sgu_ln_b,
      sgu_w_bf, sgu_bt)


def _merge_body(x_ref, attc_ref, attl_ref, conv_ref, pool_ref, sgu_ref, mod_ref, g1_ref, g2_ref,
                wg_ref, wb_ref, wo_ref, wrh_ref, wrl_ref, xm_ref, h2_ref, afft_ref, *, tm):
    i = pl.program_id(0)
    seg = _segment_of_row(i * tm)
    x = x_ref[...]
    h = _norm_modulate(x, g1_ref[...], _mod_row(mod_ref, seg, 0), _mod_row(mod_ref, seg, 1)).astype(BF16)
    att = jnp.where(i < N_CTX // tm, attc_ref[...], attl_ref[...])
    branches = (att, conv_ref[...], pool_ref[...], sgu_ref[...])
    merged = jnp.zeros((tm, D_MODEL), F32)
    for n in range(N_BRANCH):
        z_gate = _dot(h, wg_ref[:, n * D_MODEL:(n + 1) * D_MODEL])
        merged = merged + _sigmoid(z_gate) * _dot(branches[n], wb_ref[n])
    xm = x + _mod_row(mod_ref, seg, 2) * _dot(merged.astype(BF16), wo_ref[...])
    xm_ref[...] = xm
    h2 = _norm_modulate(xm, g2_ref[...], _mod_row(mod_ref, seg, 3), _mod_row(mod_ref, seg, 4))
    h2_hi = h2.astype(BF16)
    h2_ref[...] = h2_hi
    h2_lo = (h2 - h2_hi.astype(F32)).astype(BF16)
    logits = _dot(h2_hi, wrh_ref[...]) + (_dot(h2_lo, wrh_ref[...]) + _dot(h2_hi, wrl_ref[...]))
    lane = lax.broadcasted_iota(jnp.int32, logits.shape, 1)
    logits = jnp.where(lane < N_EXPERTS, logits, NEG_INF)
    e = jnp.exp(logits - jnp.max(logits, axis=-1, keepdims=True))
    aff = e / jnp.sum(e, axis=-1, keepdims=True)
    afft_ref[...] = aff.T[:N_EXPERTS, :]


def _merge(x, att_ctx, att_lat, o_conv, o_pool, o_sgu, mod, norm1_g, norm2_g, w_in_bf, w_branch_bf,
           w_out_bf, w_router_hi, w_router_lo, l):
    tm = 512
    n_ctx_tiles = N_CTX // tm
    n_lat_tiles = N_LAT // tm

    def rows(w):
        return pl.BlockSpec((tm, w), lambda i: (i, 0))

    def vec():
        return pl.BlockSpec((None, 1, D_MODEL), lambda i: (l, 0, 0))

    def const(shape, index_map):
        return pl.BlockSpec(shape, index_map, pipeline_mode=pl.Buffered(1))

    return pl.pallas_call(
        functools.partial(_merge_body, tm=tm),
        grid=(N_TOK // tm,),
        in_specs=[
            rows(D_MODEL),
            pl.BlockSpec((tm, W_ATT), lambda i: (jnp.minimum(i, n_ctx_tiles - 1), 0)),
            pl.BlockSpec((tm, W_ATT), lambda i: (jnp.clip(i - n_ctx_tiles, 0, n_lat_tiles - 1), 0)),
            rows(BRANCH_W), rows(BRANCH_W), rows(BRANCH_W),
            pl.BlockSpec((None, 8, 6 * D_MODEL), lambda i: (l, 0, 0)),
            vec(), vec(),
            const((None, D_MODEL, N_BRANCH * D_MODEL), lambda i: (l, 0, 1)),
            const((None, N_BRANCH, BRANCH_W, D_MODEL), lambda i: (l, 0, 0, 0)),
            const((None, D_MODEL, D_MODEL), lambda i: (l, 0, 0)),
            const((None, D_MODEL, LANES), lambda i: (l, 0, 0)),
            const((None, D_MODEL, LANES), lambda i: (l, 0, 0)),
        ],
        out_specs=[rows(D_MODEL), rows(D_MODEL), pl.BlockSpec((N_EXPERTS, tm), lambda i: (0, i))],
        out_shape=[
            jax.ShapeDtypeStruct((N_TOK, D_MODEL), F32),
            jax.ShapeDtypeStruct((N_TOK, D_MODEL), BF16),
            jax.ShapeDtypeStruct((N_EXPERTS, N_TOK), F32),
        ],
        compiler_params=_cparams(1, 52),
        name="merge",
    )(x, att_ctx, att_lat, o_conv, o_pool, o_sgu, mod, norm1_g, norm2_g, w_in_bf, w_branch_bf,
      w_out_bf, w_router_hi, w_router_lo)


def _route_body(afft_ref, tri_ref, pos_ref, gate_ref, *, n_seq, seq_len, cap):
    a = jnp.concatenate([afft_ref[:, s * seq_len:(s + 1) * seq_len] for s in range(n_seq)], axis=0)
    n_rows = n_seq * N_EXPERTS
    capf = float(cap)

    def count(mask):
        return jnp.sum(jnp.where(mask, 1.0, 0.0), axis=1, keepdims=True)

    def as_float(bits):
        return lax.bitcast_convert_type(bits, F32)

    thr = jnp.zeros((n_rows, 1), jnp.int32)
    for bit in range(30, -1, -1):
        cand = thr | (1 << bit)
        thr = jnp.where(count(a >= as_float(cand)) >= capf, cand, thr)
    above = a >= as_float(thr + 1)
    tie = jnp.logical_and(a >= as_float(thr), jnp.logical_not(above))
    need = capf - count(above)

    idx = lax.broadcasted_iota(jnp.int32, (n_rows, seq_len), 1)
    last = jnp.zeros((n_rows, 1), jnp.int32)
    for bit in range(seq_len.bit_length() - 2, -1, -1):
        cand = last | (1 << bit)
        before = jnp.sum(jnp.where(tie, jnp.where(idx < cand, 1.0, 0.0), 0.0), axis=1, keepdims=True)
        last = jnp.where(before < need, cand, last)
    keep = jnp.where(above, 1.0, jnp.where(tie, jnp.where(idx <= last, 1.0, 0.0), 0.0))

    blk = tri_ref.shape[0]
    offset = jnp.zeros((n_rows, 1), F32)
    pos = []
    for c in range(seq_len // blk):
        kb = keep[:, c * blk:(c + 1) * blk]
        pos.append(_dot(kb.astype(BF16), tri_ref[...]) + offset)
        offset = offset + jnp.sum(kb, axis=1, keepdims=True)
    pos = jnp.concatenate(pos, axis=1)
    pos_ref[...] = jnp.where(keep > 0.0, pos, -1.0)
    gate_ref[...] = a


def _route(aff_t, tri, n_seq, seq_len, cap, col0):
    n_rows = n_seq * N_EXPERTS
    width = n_seq * seq_len
    shape = jax.ShapeDtypeStruct((n_rows, seq_len), F32)
    spec = pl.BlockSpec((n_rows, seq_len), lambda i: (0, 0))
    return pl.pallas_call(
        functools.partial(_route_body, n_seq=n_seq, seq_len=seq_len, cap=cap),
        grid=(1,),
        in_specs=[
            pl.BlockSpec((N_EXPERTS, width), lambda i: (0, col0 // width)),
            pl.BlockSpec(tri.shape, lambda i: (0, 0)),
        ],
        out_specs=[spec, spec],
        out_shape=[shape, shape],
        compiler_params=_cparams(1, 32),
        name="route_ctx" if cap == CAP_CTX else "route_lat",
    )(aff_t, tri)


def _slot_onehot(pos_ref, cap, values_ref=None):
    n_tok = pos_ref.shape[1]
    slot = lax.broadcasted_iota(jnp.int32, (cap, n_tok), 0).astype(F32)
    rows = []
    for e in range(N_EXPERTS):
        hit = pos_ref[e:e + 1, :] == slot
        val = 1.0 if values_ref is None else values_ref[e:e + 1, :]
        rows.append(jnp.where(hit, val, 0.0).astype(BF16))
    return jnp.concatenate(rows, axis=0)


def _gather_ctx_body(pos_ref, h_ref, o_ref):
    onehot = _slot_onehot(pos_ref, CAP_CTX)
    o_ref[...] = _dot(onehot, h_ref[...]).astype(BF16).reshape(N_EXPERTS, CAP_CTX, D_MODEL)


def _gather_ctx(pos, h2):
    return pl.pallas_call(
        _gather_ctx_body,
        grid=(BATCH,),
        in_specs=[
            pl.BlockSpec((N_EXPERTS, SEQ), lambda s: (s, 0)),
            pl.BlockSpec((SEQ, D_MODEL), lambda s: (s, 0)),
        ],
        out_specs=pl.BlockSpec((N_EXPERTS, CAP_CTX, D_MODEL), lambda s: (0, s, 0)),
        out_shape=jax.ShapeDtypeStruct((N_EXPERTS, BATCH * CAP_CTX, D_MODEL), BF16),
        compiler_params=_cparams(1, 32),
        name="gather_ctx",
    )(pos, h2)


def _gather_lat_body(pos_ref, h_ref, o_ref):
    e = pl.program_id(1)
    slot = lax.broadcasted_iota(jnp.int32, (CAP_LAT, DEC_SEQ), 0).astype(F32)
    onehot = jnp.where(pos_ref[pl.ds(e, 1), :] == slot, 1.0, 0.0).astype(BF16)
    o_ref[...] = _dot(onehot, h_ref[...]).astype(BF16)


def _gather_lat(pos, h2):
    blk0 = N_CTX // DEC_SEQ
    return pl.pallas_call(
        _gather_lat_body,
        grid=(DEC_BATCH, N_EXPERTS),
        in_specs=[
            pl.BlockSpec((N_EXPERTS, DEC_SEQ), lambda b, e: (b, 0)),
            pl.BlockSpec((DEC_SEQ, D_MODEL), lambda b, e: (blk0 + b, 0)),
        ],
        out_specs=pl.BlockSpec((None, CAP_LAT, D_MODEL), lambda b, e: (e, b, 0)),
        out_shape=jax.ShapeDtypeStruct((N_EXPERTS, DEC_BATCH * CAP_LAT, D_MODEL), BF16),
        compiler_params=_cparams(2, 40),
        name="gather_lat",
    )(pos, h2)


def _ffn_body(xc_ref, xl_ref, w1_ref, w3_ref, w2_ref, yc_ref, yl_ref):
    w1 = w1_ref[...].astype(BF16)
    w3 = w3_ref[...].astype(BF16)
    w2 = w2_ref[...].astype(BF16)
    rows = 512
    for x_ref, y_ref in ((xc_ref, yc_ref), (xl_ref, yl_ref)):
        for r0 in range(0, x_ref.shape[0], rows):
            x = x_ref[r0:r0 + rows, :]
            a = _dot(x, w1)
            act = (a * _sigmoid(a) * _dot(x, w3)).astype(BF16)
            y_ref[r0:r0 + rows, :] = _dot(act, w2).astype(BF16)


def _ffn(xg_ctx, xg_lat, w1, w3, w2, l):
    n_c, n_l = xg_ctx.shape[1], xg_lat.shape[1]

    def wspec():
        return pl.BlockSpec((None, None, D_MODEL, D_MODEL), lambda e: (l, e, 0, 0))

    def xspec(n):
        return pl.BlockSpec((None, n, D_MODEL), lambda e: (e, 0, 0))

    return pl.pallas_call(
        _ffn_body,
        grid=(N_EXPERTS,),
        in_specs=[xspec(n_c), xspec(n_l), wspec(), wspec(), wspec()],
        out_specs=[xspec(n_c), xspec(n_l)],
        out_shape=[jax.ShapeDtypeStruct(xg_ctx.shape, BF16), jax.ShapeDtypeStruct(xg_lat.shape, BF16)],
        compiler_params=_cparams(1, 56),
        name="ffn",
    )(xg_ctx, xg_lat, w1, w3, w2)


def _combine_body(pos_ref, gate_ref, y_ref, x_ref, mod_ref, o_ref, *, cap, seg0):
    weights = _slot_onehot(pos_ref, cap, gate_ref)
    y = y_ref[...].reshape(N_EXPERTS * cap, D_MODEL)
    seg = seg0 + pl.program_id(0) if seg0 else 0
    o_ref[...] = x_ref[...] + _mod_row(mod_ref, seg, 5) * _dot_tn(weights, y)


def _combine_ctx(pos, gate, y, x_mid, mod, l):
    return pl.pallas_call(
        functools.partial(_combine_body, cap=CAP_CTX, seg0=0),
        grid=(BATCH,),
        in_specs=[
            pl.BlockSpec((N_EXPERTS, SEQ), lambda s: (s, 0)),
            pl.BlockSpec((N_EXPERTS, SEQ), lambda s: (s, 0)),
            pl.BlockSpec((N_EXPERTS, CAP_CTX, D_MODEL), lambda s: (0, s, 0)),
            pl.BlockSpec((SEQ, D_MODEL), lambda s: (s, 0)),
            pl.BlockSpec((None, 8, 6 * D_MODEL), lambda s: (l, 0, 0)),
        ],
        out_specs=pl.BlockSpec((SEQ, D_MODEL), lambda s: (s, 0)),
        out_shape=jax.ShapeDtypeStruct((N_TOK, D_MODEL), F32),
        input_output_aliases={3: 0},
        compiler_params=_cparams(1, 32),
        name="combine_ctx",
    )(pos, gate, y, x_mid, mod)


def _combine_lat(pos, gate, y, x_mid, mod, l):
    tiles = DEC_SEQ // CHUNK
    blk0 = N_CTX // CHUNK
    return pl.pallas_call(
        functools.partial(_combine_body, cap=CAP_LAT, seg0=1),
        grid=(DEC_BATCH, tiles),
        in_specs=[
            pl.BlockSpec((N_EXPERTS, CHUNK), lambda b, t: (b, t)),
            pl.BlockSpec((N_EXPERTS, CHUNK), lambda b, t: (b, t)),
            pl.BlockSpec((N_EXPERTS, CAP_LAT, D_MODEL), lambda b, t: (0, b, 0)),
            pl.BlockSpec((CHUNK, D_MODEL), lambda b, t: (blk0 + b * tiles + t, 0)),
            pl.BlockSpec((None, 8, 6 * D_MODEL), lambda b, t: (l, 0, 0)),
        ],
        out_specs=pl.BlockSpec((CHUNK, D_MODEL), lambda b, t: (blk0 + b * tiles + t, 0)),
        out_shape=jax.ShapeDtypeStruct((N_TOK, D_MODEL), F32),
        input_output_aliases={3: 0},
        compiler_params=_cparams(2, 48),
        name="combine_lat",
    )(pos, gate, y, x_mid, mod)


def _final_norm_body(x_ref, g_ref, o_ref):
    x = x_ref[...]
    o_ref[...] = x * lax.rsqrt(jnp.mean(x * x, axis=-1, keepdims=True) + EPS) * g_ref[...]


def _final_norm(x, g, row0, n_rows):
    tm = 512
    return pl.pallas_call(
        _final_norm_body,
        grid=(n_rows // tm,),
        in_specs=[
            pl.BlockSpec((tm, D_MODEL), lambda i: (row0 // tm + i, 0)),
            pl.BlockSpec((1, D_MODEL), lambda i: (0, 0)),
        ],
        out_specs=pl.BlockSpec((tm, D_MODEL), lambda i: (i, 0)),
        out_shape=jax.ShapeDtypeStruct((n_rows, D_MODEL), F32),
        compiler_params=_cparams(1, 32),
        name="final_norm",
    )(x, g)


def kernel(x_prompt, x_sample, cache_k, cache_v, c, c_ctx, w_ada, b_ada, norm1_g, w_in, rpb, conv_w, conv_b,
           conv_ln_g, conv_ln_b, pool_w, pool_scale, sgu_ln_g, sgu_ln_b, sgu_w, sgu_b, w_branch, w_out,
           norm2_g, w_router, w1, w3, w2, final_g):
    def vecs(a):
        return a.reshape(DEPTH, 1, a.shape[-1])

    cond8 = jnp.concatenate([c_ctx[None, :], c, jnp.zeros((8 - 1 - DEC_BATCH, D_MODEL), F32)], axis=0)
    mod = _adaln(cond8, w_ada, b_ada)
    table = _rpb_table(rpb)
    cos_np, sin_np = _rope_tables()
    cos, sin = jnp.asarray(cos_np), jnp.asarray(sin_np)
    tri = jnp.asarray(np.triu(np.ones((CHUNK, CHUNK), np.float32), 1), BF16)

    w_in_bf = w_in.astype(BF16)
    w_branch_bf = w_branch.astype(BF16)
    w_out_bf = w_out.astype(BF16)
    pool_w_bf = pool_w.astype(BF16)
    sgu_w_bf = sgu_w.astype(BF16)
    sgu_bt = sgu_b.transpose(0, 2, 1)
    w_router_pad = jnp.pad(w_router, ((0, 0), (0, 0), (0, LANES - N_EXPERTS)))
    w_router_hi = w_router_pad.astype(BF16)
    w_router_lo = (w_router_pad - w_router_hi.astype(F32)).astype(BF16)
    norm1_v, norm2_v = vecs(norm1_g), vecs(norm2_g)
    conv_b_v, conv_ln_g_v, conv_ln_b_v = vecs(conv_b), vecs(conv_ln_g), vecs(conv_ln_b)
    pool_scale_v, sgu_ln_g_v, sgu_ln_b_v = vecs(pool_scale), vecs(sgu_ln_g), vecs(sgu_ln_b)

    x = jnp.concatenate([x_prompt.reshape(N_CTX, D_MODEL), x_sample.reshape(N_LAT, D_MODEL)], axis=0)
    new_k = jnp.zeros((BATCH, DEPTH, N_HEADS, SEQ, HEAD_DIM), F32)
    new_v = jnp.zeros((BATCH, DEPTH, N_HEADS, SEQ, HEAD_DIM), F32)

    for l in range(DEPTH):
        proj = _proj(x, norm1_v, mod, w_in_bf, l)
        att_ctx, new_k, new_v = _attn_ctx(proj, new_k, new_v, l)
        att_lat = _attn_lat(proj, cache_k, cache_v, table, cos, sin, l)
        o_conv, o_pool, o_sgu = _seqmix(proj, conv_w, conv_b_v, conv_ln_g_v, conv_ln_b_v, pool_w_bf,
                                        pool_scale_v, sgu_ln_g_v, sgu_ln_b_v, sgu_w_bf, sgu_bt, l)
        x_mid, h2, aff_t = _merge(x, att_ctx, att_lat, o_conv, o_pool, o_sgu, mod, norm1_v, norm2_v,
                                  w_in_bf, w_branch_bf, w_out_bf, w_router_hi, w_router_lo, l)
        pos_c, gate_c = _route(aff_t, tri, BATCH, SEQ, CAP_CTX, 0)
        pos_l, gate_l = _route(aff_t, tri, DEC_BATCH, DEC_SEQ, CAP_LAT, N_CTX)
        y_c, y_l = _ffn(_gather_ctx(pos_c, h2), _gather_lat(pos_l, h2), w1, w3, w2, l)
        x = _combine_ctx(pos_c, gate_c, y_c, x_mid, mod, l)
        x = _combine_lat(pos_l, gate_l, y_l, x, mod, l)

    y_prompt = _final_norm(x, final_g[None, :], 0, N_CTX).reshape(BATCH, SEQ, D_MODEL)
    y_sample = _final_norm(x, final_g[None, :], N_CTX, N_LAT).reshape(DEC_BATCH, DEC_SEQ, D_MODEL)
    return (y_prompt, y_sample, new_k, new_v)
```

```python
import functools

import numpy as np
import jax
import jax.numpy as jnp
from jax import lax
from jax.experimental import pallas as pl
from jax.experimental.pallas import tpu as pltpu

F32 = jnp.float32
BF16 = jnp.bfloat16

D_MODEL = 1024
BATCH = 32
SEQ = 256
DEPTH = 4
DEC_BATCH = 2
DEC_SEQ = 2048
PAST_LEN = 512
GRID_W = 64
GRID_ROWS = DEC_SEQ // GRID_W
N_HEADS = 8
HEAD_DIM = 64
W_ATT = N_HEADS * HEAD_DIM
NA_ROWS = 8
NA_COLS = 16
ROPE_BASE = 10000.0
W_CONV = 512
CONV_K = 31
W_POOL = 512
POOL_WINDOWS = (2, 4, 8, 16)
POOL_GROUP = 128
W_SGU = 512
SGU_GROUPS = 4
SGU_CHUNK = 128
N_BRANCH = 4
BRANCH_W = 512
N_EXPERTS = 16
EC_CAPACITY = 2
EPS = 1e-6
NEG_INF = -1e30

N_CTX = BATCH * SEQ
N_LAT = DEC_BATCH * DEC_SEQ
N_TOK = N_CTX + N_LAT
D_PROJ = 4096
COL_BLK = 512
CAP_CTX = EC_CAPACITY * SEQ // N_EXPERTS
CAP_LAT = EC_CAPACITY * DEC_SEQ // N_EXPERTS
N_ROW_OFF = 2 * NA_ROWS
Q_ROWS = 4
KEY_ROWS = 12
HEAD_GROUP = 4
HALO = 16
CHUNK = 256
LANES = 128
SUBLANES = 8
VMEM_LIMIT_CAP = 56 * 1024 * 1024

HIGHEST = lax.Precision.HIGHEST


def _cparams(n_grid, vmem_mb):
    return pltpu.CompilerParams(
        dimension_semantics=("arbitrary",) * n_grid,
        vmem_limit_bytes=min(vmem_mb * 1024 * 1024, VMEM_LIMIT_CAP))


def _dot(a, b):
    return jnp.dot(a, b, preferred_element_type=F32)


def _dot_tn(a, b):
    return lax.dot_general(a, b, (((0,), (0,)), ((), ())), preferred_element_type=F32)


def _sigmoid(x):
    return 0.5 * jnp.tanh(0.5 * x) + 0.5


def _segment_of_row(row0):
    return jnp.where(row0 < N_CTX, 0, 1 + (row0 - N_CTX) // DEC_SEQ)


def _mod_row(mod_ref, seg, k):
    return mod_ref[pl.ds(seg, 1), k * D_MODEL:(k + 1) * D_MODEL]


def _norm_modulate(x, g, shift, scale):
    y = x * lax.rsqrt(jnp.mean(x * x, axis=-1, keepdims=True) + EPS) * g
    return y * (1.0 + scale) + shift


def _layernorm(x, g, b):
    mu = jnp.mean(x, axis=-1, keepdims=True)
    xc = x - mu
    var = jnp.mean(xc * xc, axis=-1, keepdims=True)
    return xc * lax.rsqrt(var + EPS) * g + b


def _adaln_body(cond_ref, w_ref, b_ref, o_ref):
    c = cond_ref[...]
    a = (c * _sigmoid(c)).astype(BF16)
    o_ref[...] = _dot(a, w_ref[...].astype(BF16)) + b_ref[...]


def _adaln(cond8, w_ada, b_ada):
    tn = 1536
    n6 = 6 * D_MODEL
    return pl.pallas_call(
        _adaln_body,
        grid=(DEPTH, n6 // tn),
        in_specs=[
            pl.BlockSpec((8, D_MODEL), lambda l, j: (0, 0)),
            pl.BlockSpec((None, D_MODEL, tn), lambda l, j: (l, 0, j)),
            pl.BlockSpec((None, 1, tn), lambda l, j: (l, 0, j)),
        ],
        out_specs=pl.BlockSpec((None, 8, tn), lambda l, j: (l, 0, j)),
        out_shape=jax.ShapeDtypeStruct((DEPTH, 8, n6), F32),
        compiler_params=_cparams(2, 32),
        name="adaln",
    )(cond8, w_ada, b_ada.reshape(DEPTH, 1, n6))


def _rpb_onehots():
    cq = np.arange(GRID_W)[None, :]
    kc = np.arange(GRID_W)[:, None]
    start = np.clip(cq - NA_COLS // 2, 0, GRID_W - NA_COLS)
    valid = (kc >= start) & (kc < start + NA_COLS)
    d_col = np.clip(kc - cq, -(NA_COLS - 1), NA_COLS - 1) + (NA_COLS - 1)
    left = np.zeros((32, GRID_W, 2 * GRID_W), np.float32)
    right = np.zeros((32, GRID_W, 2 * GRID_W), np.float32)
    mask = np.zeros((GRID_W, 2 * GRID_W), np.float32)
    for j in range(2 * NA_COLS - 1):
        hit = (d_col == j) & valid
        left[j, :, :GRID_W] = hit
        right[j, :, GRID_W:] = hit
    mask[:, :GRID_W] = np.where(valid, 0.0, NEG_INF)
    mask[:, GRID_W:] = np.where(valid, 0.0, NEG_INF)
    n = GRID_W * 2 * GRID_W
    return left.reshape(32, n), right.reshape(32, n), mask.reshape(1, n)


def _rpb_body(a_ref, b_ref, left_ref, right_ref, mask_ref, o_ref):
    t = jnp.dot(a_ref[...], left_ref[...], precision=HIGHEST, preferred_element_type=F32)
    t = t + jnp.dot(b_ref[...], right_ref[...], precision=HIGHEST, preferred_element_type=F32)
    o_ref[...] = t + mask_ref[...]


def _rpb_table(rpb):
    left, right, mask = _rpb_onehots()
    n_rows = N_HEADS * N_ROW_OFF
    rpb_p = jnp.pad(rpb, ((0, 0), (0, 0), (0, 0), (0, 1)))
    a = jnp.pad(rpb_p, ((0, 0), (0, 0), (0, 1), (0, 0))).reshape(DEPTH, n_rows, 32)
    b = jnp.pad(rpb_p, ((0, 0), (0, 0), (1, 0), (0, 0))).reshape(DEPTH, n_rows, 32)
    n = left.shape[1]
    out = pl.pallas_call(
        _rpb_body,
        grid=(DEPTH,),
        in_specs=[
            pl.BlockSpec((None, n_rows, 32), lambda l: (l, 0, 0)),
            pl.BlockSpec((None, n_rows, 32), lambda l: (l, 0, 0)),
            pl.BlockSpec((32, n), lambda l: (0, 0)),
            pl.BlockSpec((32, n), lambda l: (0, 0)),
            pl.BlockSpec((1, n), lambda l: (0, 0)),
        ],
        out_specs=pl.BlockSpec((None, n_rows, n), lambda l: (l, 0, 0)),
        out_shape=jax.ShapeDtypeStruct((DEPTH, n_rows, n), F32),
        compiler_params=_cparams(1, 32),
        name="rpb_table",
    )(a, b, jnp.asarray(left), jnp.asarray(right), jnp.asarray(mask))
    return out.reshape(DEPTH, N_HEADS, N_ROW_OFF, GRID_W, 2 * GRID_W)


def _proj_body(x_ref, g_ref, mod_ref, w_ref, o_ref, h_scr, *, tm):
    i = pl.program_id(0)

    @pl.when(pl.program_id(1) == 0)
    def _():
        seg = _segment_of_row(i * tm)
        h = _norm_modulate(x_ref[...], g_ref[...], _mod_row(mod_ref, seg, 0), _mod_row(mod_ref, seg, 1))
        h_scr[...] = h.astype(BF16)

    o_ref[...] = _dot(h_scr[...], w_ref[...]).astype(BF16)


def _proj(x, norm_g, mod, w_in_bf, l):
    tm, tn = 1024, 1024
    return pl.pallas_call(
        functools.partial(_proj_body, tm=tm),
        grid=(N_TOK // tm, D_PROJ // tn),
        in_specs=[
            pl.BlockSpec((tm, D_MODEL), lambda i, j: (i, 0)),
            pl.BlockSpec((None, 1, D_MODEL), lambda i, j: (l, 0, 0)),
            pl.BlockSpec((None, 8, 6 * D_MODEL), lambda i, j: (l, 0, 0)),
            pl.BlockSpec((None, D_MODEL, tn), lambda i, j: (l, 0, j)),
        ],
        out_specs=pl.BlockSpec((tm, tn), lambda i, j: (i, j)),
        out_shape=jax.ShapeDtypeStruct((N_TOK, D_PROJ), BF16),
        scratch_shapes=[pltpu.VMEM((tm, D_MODEL), BF16)],
        compiler_params=_cparams(2, 40),
        name="proj",
    )(x, norm_g, mod, w_in_bf)


def _head_rows_in_pair(x_t, h):
    rows = x_t[h * HEAD_DIM:(h + 1) * HEAD_DIM, :]
    zeros = jnp.zeros_like(rows)
    return jnp.concatenate([rows, zeros] if h % 2 == 0 else [zeros, rows], axis=0)


def _pair_lanes(h):
    return slice((h // 2) * LANES, (h // 2 + 1) * LANES)


def _attn_ctx_body(q_ref, k_ref, v_ref, o_ref):
    k = k_ref[...]
    v = v_ref[...]
    q_t = (q_ref[...].astype(F32) * (HEAD_DIM ** -0.5)).T.astype(BF16)
    v_t = v.astype(F32).T.astype(BF16)
    heads = range(N_HEADS)
    s_t = [_dot(k[:, _pair_lanes(h)], _head_rows_in_pair(q_t, h)) for h in heads]
    e_t = [jnp.exp(s - jnp.max(s, axis=0, keepdims=True)) for s in s_t]
    denom = [jnp.sum(e, axis=0, keepdims=True) for e in e_t]
    o_t = [_dot(v_t[h * HEAD_DIM:(h + 1) * HEAD_DIM, :], e_t[h].astype(BF16)) / denom[h] for h in heads]
    o_ref[...] = jnp.concatenate(o_t, axis=0).T.astype(BF16)


def _attn_ctx(proj):
    return pl.pallas_call(
        _attn_ctx_body,
        grid=(BATCH,),
        in_specs=[
            pl.BlockSpec((SEQ, COL_BLK), lambda b: (b, 0)),
            pl.BlockSpec((SEQ, COL_BLK), lambda b: (b, 1)),
            pl.BlockSpec((SEQ, COL_BLK), lambda b: (b, 2)),
        ],
        out_specs=pl.BlockSpec((SEQ, W_ATT), lambda b: (b, 0)),
        out_shape=jax.ShapeDtypeStruct((N_CTX, W_ATT), BF16),
        compiler_params=_cparams(1, 32),
        name="attn_ctx",
    )(proj, proj, proj)


def _kv_out_body(*refs):
    kv_refs, (ck_ref, cv_ref) = refs[:2 * DEPTH], refs[2 * DEPTH:]
    l = pl.program_id(0)
    for j in range(DEPTH):
        @pl.when(l == j)
        def _(j=j):
            k = kv_refs[2 * j][...]
            v = kv_refs[2 * j + 1][...]
            for h in range(N_HEADS):
                sl = slice(h * HEAD_DIM, (h + 1) * HEAD_DIM)
                ck_ref[h] = k[:, sl].astype(F32)
                cv_ref[h] = v[:, sl].astype(F32)


def _kv_out(projs):
    def src(j, col):
        return pl.BlockSpec((SEQ, COL_BLK), lambda l, b: (jnp.where(l == j, b, jnp.where(l < j, 0, BATCH - 1)), col))

    cache_spec = pl.BlockSpec((None, None, N_HEADS, SEQ, HEAD_DIM), lambda l, b: (b, l, 0, 0, 0))
    cache_shape = jax.ShapeDtypeStruct((BATCH, DEPTH, N_HEADS, SEQ, HEAD_DIM), F32)
    in_specs, args = [], []
    for j in range(DEPTH):
        in_specs += [src(j, 1), src(j, 2)]
        args += [projs[j], projs[j]]
    return pl.pallas_call(
        _kv_out_body,
        grid=(DEPTH, BATCH),
        in_specs=in_specs,
        out_specs=[cache_spec, cache_spec],
        out_shape=[cache_shape, cache_shape],
        compiler_params=_cparams(2, 32),
        name="kv_out",
    )(*args)


def _rope_tables():
    t = np.arange(DEC_SEQ)
    half = HEAD_DIM // 2
    nf = half // 2
    inv = (1.0 / (ROPE_BASE ** (np.arange(nf) / nf))).astype(np.float32)
    cos = np.zeros((DEC_SEQ, HEAD_DIM), np.float32)
    sin = np.zeros((DEC_SEQ, HEAD_DIM), np.float32)
    for blk, pos in enumerate((t // GRID_W, t % GRID_W)):
        ang = (pos[:, None].astype(np.float32) * inv[None, :]).astype(np.float32)
        c, s = np.cos(ang), np.sin(ang)
        cos[:, blk * half:(blk + 1) * half] = np.concatenate([c, c], axis=1)
        sin[:, blk * half:(blk + 1) * half] = np.concatenate([-s, s], axis=1)
    return np.tile(cos, (1, 2)), np.tile(sin, (1, 2))


def _rope(x, cos, sin):
    cos = jnp.concatenate([cos] * (W_ATT // LANES), axis=1)
    sin = jnp.concatenate([sin] * (W_ATT // LANES), axis=1)
    lane = lax.broadcasted_iota(jnp.int32, x.shape, 1)
    nf = HEAD_DIM // 4
    partner = jnp.where(lane % (2 * nf) < nf, pltpu.roll(x, W_ATT - nf, 1), pltpu.roll(x, nf, 1))
    return x * cos + partner * sin


def _attn_lat_body(q_ref, k_ref, v_ref, ck_ref, cv_ref, tab_ref, cos_ref, sin_ref, o_ref,
                   krot_scr, vt_scr, ck_scr, cvt_scr):
    g = pl.program_id(1)
    blk = Q_ROWS * GRID_W
    n_blk = DEC_SEQ // blk

    @pl.when(g == 0)
    def _():
        def chunk(c, carry):
            rows = pl.ds(pl.multiple_of(c * blk, blk), blk)
            kr = _rope(k_ref[rows, :].astype(F32), cos_ref[rows, :], sin_ref[rows, :])
            krot_scr[rows, :] = kr.astype(BF16)
            vt_scr[c] = v_ref[rows, :].astype(F32).T.astype(BF16)
            return carry
        lax.fori_loop(0, n_blk, chunk, 0)
        for h in range(0, N_HEADS, 2):
            ck_scr[h] = ck_ref[h].astype(BF16)
            ck_scr[h + 1] = ck_ref[h + 1].astype(BF16)
            pair_t = jnp.concatenate([cv_ref[h], cv_ref[h + 1]], axis=1).T
            cvt_scr[h] = pair_t[:HEAD_DIM, :].astype(BF16)
            cvt_scr[h + 1] = pair_t[HEAD_DIM:, :].astype(BF16)

    kh = min(NA_ROWS, GRID_ROWS)
    scale = HEAD_DIM ** -0.5
    r0 = g * Q_ROWS
    blk0 = jnp.clip(g - 1, 0, n_blk - KEY_ROWS // Q_ROWS)
    u = blk0 * Q_ROWS

    qrows = pl.ds(pl.multiple_of(g * blk, blk), blk)
    q = q_ref[...].astype(F32)
    q_rot_t = (_rope(q, cos_ref[qrows, :], sin_ref[qrows, :]) * scale).T.astype(BF16)
    q_plain_t = (q * scale).T.astype(BF16)
    k_win = krot_scr[pl.ds(pl.multiple_of(blk0 * blk, blk), KEY_ROWS * GRID_W), :]

    lane = lax.broadcasted_iota(jnp.int32, (GRID_W, 2 * GRID_W), 1)
    row_masks = []
    for kr in range(KEY_ROWS):
        pair_masks = []
        for ip in range(Q_ROWS // 2):
            m = []
            for i in (2 * ip, 2 * ip + 1):
                start = jnp.clip(r0 + i - kh // 2, 0, GRID_ROWS - kh)
                inside = jnp.logical_and(u + kr >= start, u + kr < start + kh)
                m.append(jnp.where(inside, 0.0, NEG_INF))
            pair_masks.append(jnp.where(lane < GRID_W, m[0], m[1]))
        row_masks.append(jnp.concatenate(pair_masks, axis=1))

    def head_rows(h):
        return slice(h * HEAD_DIM, (h + 1) * HEAD_DIM)

    def local_bias(h):
        bias_rows = []
        for kr in range(KEY_ROWS):
            tiles = []
            for ip in range(Q_ROWS // 2):
                d_row = u + kr - (r0 + 2 * ip) + (NA_ROWS - 1)
                tiles.append(tab_ref[h, jnp.clip(d_row, 0, N_ROW_OFF - 1)])
            bias_rows.append(jnp.concatenate(tiles, axis=1) + row_masks[kr])
        return jnp.concatenate(bias_rows, axis=0)

    outs = []
    for h0 in range(0, N_HEADS, HEAD_GROUP):
        heads = range(h0, h0 + HEAD_GROUP)
        s_loc = [_dot(k_win[:, _pair_lanes(h)], _head_rows_in_pair(q_rot_t, h)) + local_bias(h)
                 for h in heads]
        s_ctx = [_dot(ck_scr[h], q_plain_t[head_rows(h), :]) for h in heads]
        m = [jnp.maximum(jnp.max(a, axis=0, keepdims=True), jnp.max(b, axis=0, keepdims=True))
             for a, b in zip(s_loc, s_ctx)]
        e_loc = [jnp.exp(a - mm) for a, mm in zip(s_loc, m)]
        e_ctx = [jnp.exp(b - mm) for b, mm in zip(s_ctx, m)]
        denom = [jnp.sum(a, axis=0, keepdims=True) + jnp.sum(b, axis=0, keepdims=True)
                 for a, b in zip(e_loc, e_ctx)]
        for i, h in enumerate(heads):
            p_loc = e_loc[i].astype(BF16)
            o_t = _dot(cvt_scr[h], e_ctx[i].astype(BF16))
            for c in range(KEY_ROWS // Q_ROWS):
                o_t = o_t + _dot(vt_scr[blk0 + c, head_rows(h), :], p_loc[c * blk:(c + 1) * blk, :])
            outs.append(o_t / denom[i])
    o_ref[...] = jnp.concatenate(outs, axis=0).T.astype(BF16)


def _attn_lat(proj, cache_k, cache_v, table, cos, sin, l):
    blk = Q_ROWS * GRID_W
    steps = GRID_ROWS // Q_ROWS
    q_blk0 = N_CTX // blk
    kv_blk0 = N_CTX // DEC_SEQ
    cache_spec = pl.BlockSpec((None, None, N_HEADS, PAST_LEN, HEAD_DIM), lambda b, g: (b, l, 0, 0, 0))

    def const(shape, index_map):
        return pl.BlockSpec(shape, index_map, pipeline_mode=pl.Buffered(1))

    return pl.pallas_call(
        _attn_lat_body,
        grid=(DEC_BATCH, steps),
        in_specs=[
            pl.BlockSpec((blk, COL_BLK), lambda b, g: (q_blk0 + b * steps + g, 0)),
            pl.BlockSpec((DEC_SEQ, COL_BLK), lambda b, g: (kv_blk0 + b, 1)),
            pl.BlockSpec((DEC_SEQ, COL_BLK), lambda b, g: (kv_blk0 + b, 2)),
            cache_spec, cache_spec,
            const((None, N_HEADS, N_ROW_OFF, GRID_W, 2 * GRID_W), lambda b, g: (l, 0, 0, 0, 0)),
            const((DEC_SEQ, LANES), lambda b, g: (0, 0)),
            const((DEC_SEQ, LANES), lambda b, g: (0, 0)),
        ],
        out_specs=pl.BlockSpec((blk, W_ATT), lambda b, g: (b * steps + g, 0)),
        out_shape=jax.ShapeDtypeStruct((N_LAT, W_ATT), BF16),
        scratch_shapes=[
            pltpu.VMEM((DEC_SEQ, W_ATT), BF16),
            pltpu.VMEM((DEC_SEQ // blk, W_ATT, blk), BF16),
            pltpu.VMEM((N_HEADS, PAST_LEN, HEAD_DIM), BF16),
            pltpu.VMEM((N_HEADS, HEAD_DIM, PAST_LEN), BF16),
        ],
        compiler_params=_cparams(2, 48),
        name="attn_lat",
    )(proj, proj, proj, cache_k, cache_v, table, cos, sin)


def _seqmix_body(a_ref, g_ref, p_ref, ap_ref, gp_ref, pp_ref, an_ref, gn_ref, pn_ref, u_ref, v_ref,
                 cw_ref, cb_ref, clg_ref, clb_ref, pw_ref, ps_ref, slg_ref, slb_ref, sw_ref, sbt_ref,
                 oc_ref, op_ref, os_ref, ypad, zpad, yshift):
    i = pl.program_id(0)
    n_ctx_chunks = N_CTX // CHUNK
    per_seq = DEC_SEQ // CHUNK
    j = (i - n_ctx_chunks) % per_seq
    is_lat = i >= n_ctx_chunks
    left_ok = jnp.logical_and(is_lat, j > 0)
    right_ok = jnp.logical_and(is_lat, j < per_seq - 1)
    seq_len = jnp.where(is_lat, DEC_SEQ, SEQ)
    chunk_off = jnp.where(is_lat, j * CHUNK, 0)

    def glu(a, g):
        return a.astype(F32) * _sigmoid(g.astype(F32))

    ypad[0:HALO, :] = jnp.where(left_ok, glu(ap_ref[...], gp_ref[...]), 0.0)
    ypad[HALO:HALO + CHUNK, :] = glu(a_ref[...], g_ref[...])
    ypad[HALO + CHUNK:, :] = jnp.where(right_ok, glu(an_ref[...], gn_ref[...]), 0.0)
    zpad[0:HALO, :] = jnp.where(left_ok, pp_ref[...].astype(F32), 0.0)
    zpad[HALO:HALO + CHUNK, :] = p_ref[...].astype(F32)
    zpad[HALO + CHUNK:, :] = jnp.where(right_ok, pn_ref[...].astype(F32), 0.0)

    for s in range(SUBLANES):
        yshift[s] = ypad[s:s + yshift.shape[1], :]
    rows = 64
    half_k = CONV_K // 2
    conv_rows = []
    for rb in range(CHUNK // rows):
        cols = []
        for cg in range(W_CONV // LANES):
            lanes = slice(cg * LANES, (cg + 1) * LANES)
            acc = jnp.zeros((rows, LANES), F32)
            for k in range(CONV_K):
                start = HALO + rb * rows + k - half_k
                s = start % SUBLANES
                acc = acc + cw_ref[k:k + 1, lanes] * yshift[s, start - s:start - s + rows, lanes]
            cols.append(acc)
        conv_rows.append(jnp.concatenate(cols, axis=1))
    y = jnp.concatenate(conv_rows, axis=0) + cb_ref[...]
    y = _layernorm(y, clg_ref[...], clb_ref[...])
    oc_ref[...] = (y * _sigmoid(y)).astype(BF16)

    t_seq = lax.broadcasted_iota(jnp.int32, (CHUNK, 1), 0) + chunk_off
    pooled = []
    for gi, w in enumerate(POOL_WINDOWS):
        lanes = slice(gi * POOL_GROUP, (gi + 1) * POOL_GROUP)
        lo, hi = -(w // 2), w - w // 2 - 1
        tot = jnp.zeros((CHUNK, POOL_GROUP), F32)
        for jj in range(lo, hi + 1):
            tot = tot + zpad[HALO + jj:HALO + jj + CHUNK, lanes]
        cnt = w - jnp.maximum(0, -lo - t_seq) - jnp.maximum(0, t_seq + hi - (seq_len - 1))
        d = tot / cnt.astype(F32) - zpad[HALO:HALO + CHUNK, lanes]
        pooled.append(_dot(d.astype(BF16), pw_ref[gi]))
    op_ref[...] = (jnp.concatenate(pooled, axis=1) * ps_ref[...]).astype(BF16)

    u = jax.nn.gelu(u_ref[...].astype(F32))
    vn = _layernorm(jax.nn.gelu(v_ref[...].astype(F32)), slg_ref[...], slb_ref[...]).astype(BF16)
    sg_rows = []
    for n in range(CHUNK // SGU_CHUNK):
        pos = slice(n * SGU_CHUNK, (n + 1) * SGU_CHUNK)
        cols = []
        for g in range(SGU_GROUPS):
            lanes = slice(g * LANES, (g + 1) * LANES)
            cols.append(_dot(sw_ref[g], vn[pos, lanes]) + sbt_ref[:, g:g + 1])
        sg_rows.append(jnp.concatenate(cols, axis=1))
    os_ref[...] = (u * jnp.concatenate(sg_rows, axis=0)).astype(BF16)


def _seqmix(proj, conv_w, conv_b, conv_ln_g, conv_ln_b, pool_w_bf, pool_scale, sgu_ln_g, sgu_ln_b,
            sgu_w_bf, sgu_bt, l):
    n_chunks = N_TOK // CHUNK
    halos_per_chunk = CHUNK // HALO
    n_halo = N_TOK // HALO

    def main(col):
        return pl.BlockSpec((CHUNK, COL_BLK), lambda i: (i, col))

    def prev(col):
        return pl.BlockSpec((HALO, COL_BLK), lambda i: (jnp.maximum(i * halos_per_chunk - 1, 0), col))

    def nxt(col):
        return pl.BlockSpec((HALO, COL_BLK), lambda i: (jnp.minimum((i + 1) * halos_per_chunk, n_halo - 1), col))

    def vec(n):
        return pl.BlockSpec((None, 1, n), lambda i: (l, 0, 0))

    out_spec = pl.BlockSpec((CHUNK, BRANCH_W), lambda i: (i, 0))
    out_shape = jax.ShapeDtypeStruct((N_TOK, BRANCH_W), BF16)
    return pl.pallas_call(
        _seqmix_body,
        grid=(n_chunks,),
        in_specs=[
            main(3), main(4), main(5), prev(3), prev(4), prev(5), nxt(3), nxt(4), nxt(5), main(6), main(7),
            pl.BlockSpec((None, CONV_K, W_CONV), lambda i: (l, 0, 0)),
            vec(W_CONV), vec(W_CONV), vec(W_CONV),
            pl.BlockSpec((None, len(POOL_WINDOWS), POOL_GROUP, POOL_GROUP), lambda i: (l, 0, 0, 0)),
            vec(W_POOL), vec(W_SGU), vec(W_SGU),
            pl.BlockSpec((None, SGU_GROUPS, SGU_CHUNK, SGU_CHUNK), lambda i: (l, 0, 0, 0)),
            pl.BlockSpec((None, SGU_CHUNK, SGU_GROUPS), lambda i: (l, 0, 0)),
        ],
        out_specs=[out_spec, out_spec, out_spec],
        out_shape=[out_shape, out_shape, out_shape],
        scratch_shapes=[
            pltpu.VMEM((CHUNK + 2 * HALO, W_CONV), F32),
            pltpu.VMEM((CHUNK + 2 * HALO, W_POOL), F32),
            pltpu.VMEM((SUBLANES, CHUNK + 2 * HALO - SUBLANES, W_CONV), F32),
        ],
        compiler_params=_cparams(1, 32),
        name="seqmix",
    )(*([proj] * 11), conv_w, conv_b, conv_ln_g, conv_ln_b, pool_w_bf, pool_scale, sgu_ln_g, sgu_ln_b,
      sgu_w_bf, sgu_bt)


def _merge_body(x_ref, attc_ref, attl_ref, conv_ref, pool_ref, sgu_ref, mod_ref, g1_ref, g2_ref,
                wg_ref, wb_ref, wo_ref, wrh_ref, wrl_ref, xm_ref, h2_ref, afft_ref, *, tm, sub):
    i = pl.program_id(0)
    seg = _segment_of_row(i * tm)
    is_ctx = i < N_CTX // tm
    for r0 in range(0, tm, sub):
        rows = slice(r0, r0 + sub)
        x = x_ref[rows, :]
        h = _norm_modulate(x, g1_ref[...], _mod_row(mod_ref, seg, 0), _mod_row(mod_ref, seg, 1)).astype(BF16)
        att = jnp.where(is_ctx, attc_ref[rows, :], attl_ref[rows, :])
        branches = (att, conv_ref[rows, :], pool_ref[rows, :], sgu_ref[rows, :])
        merged = jnp.zeros((sub, D_MODEL), F32)
        for n in range(N_BRANCH):
            z_gate = _dot(h, wg_ref[:, n * D_MODEL:(n + 1) * D_MODEL])
            merged = merged + _sigmoid(z_gate) * _dot(branches[n], wb_ref[n])
        xm = x + _mod_row(mod_ref, seg, 2) * _dot(merged.astype(BF16), wo_ref[...])
        xm_ref[rows, :] = xm
        h2 = _norm_modulate(xm, g2_ref[...], _mod_row(mod_ref, seg, 3), _mod_row(mod_ref, seg, 4))
        h2_hi = h2.astype(BF16)
        h2_ref[rows, :] = h2_hi
        h2_lo = (h2 - h2_hi.astype(F32)).astype(BF16)
        logits = _dot(h2_hi, wrh_ref[...]) + (_dot(h2_lo, wrh_ref[...]) + _dot(h2_hi, wrl_ref[...]))
        lane = lax.broadcasted_iota(jnp.int32, logits.shape, 1)
        logits = jnp.where(lane < N_EXPERTS, logits, NEG_INF)
        e = jnp.exp(logits - jnp.max(logits, axis=-1, keepdims=True))
        aff = e / jnp.sum(e, axis=-1, keepdims=True)
        afft_ref[:, rows] = aff.T[:N_EXPERTS, :]


def _merge(x, att_ctx, att_lat, o_conv, o_pool, o_sgu, mod, norm1_g, norm2_g, w_in_bf, w_branch_bf,
           w_out_bf, w_router_hi, w_router_lo, l):
    tm = 1024
    n_ctx_tiles = N_CTX // tm
    n_lat_tiles = N_LAT // tm

    def rows(w):
        return pl.BlockSpec((tm, w), lambda i: (i, 0))

    def vec():
        return pl.BlockSpec((None, 1, D_MODEL), lambda i: (l, 0, 0))

    def const(shape, index_map):
        return pl.BlockSpec(shape, index_map, pipeline_mode=pl.Buffered(1))

    return pl.pallas_call(
        functools.partial(_merge_body, tm=tm, sub=256),
        grid=(N_TOK // tm,),
        in_specs=[
            rows(D_MODEL),
            pl.BlockSpec((tm, W_ATT), lambda i: (jnp.minimum(i, n_ctx_tiles - 1), 0)),
            pl.BlockSpec((tm, W_ATT), lambda i: (jnp.clip(i - n_ctx_tiles, 0, n_lat_tiles - 1), 0)),
            rows(BRANCH_W), rows(BRANCH_W), rows(BRANCH_W),
            pl.BlockSpec((None, 8, 6 * D_MODEL), lambda i: (l, 0, 0)),
            vec(), vec(),
            const((None, D_MODEL, N_BRANCH * D_MODEL), lambda i: (l, 0, 1)),
            const((None, N_BRANCH, BRANCH_W, D_MODEL), lambda i: (l, 0, 0, 0)),
            const((None, D_MODEL, D_MODEL), lambda i: (l, 0, 0)),
            const((None, D_MODEL, LANES), lambda i: (l, 0, 0)),
            const((None, D_MODEL, LANES), lambda i: (l, 0, 0)),
        ],
        out_specs=[rows(D_MODEL), rows(D_MODEL), pl.BlockSpec((N_EXPERTS, tm), lambda i: (0, i))],
        out_shape=[
            jax.ShapeDtypeStruct((N_TOK, D_MODEL), F32),
            jax.ShapeDtypeStruct((N_TOK, D_MODEL), BF16),
            jax.ShapeDtypeStruct((N_EXPERTS, N_TOK), F32),
        ],
        compiler_params=_cparams(1, 52),
        name="merge",
    )(x, att_ctx, att_lat, o_conv, o_pool, o_sgu, mod, norm1_g, norm2_g, w_in_bf, w_branch_bf,
      w_out_bf, w_router_hi, w_router_lo)


def _route_body(afft_ref, tri_ref, pos_ref, gate_ref, *, n_seq, seq_len, cap):
    a = jnp.concatenate([afft_ref[:, s * seq_len:(s + 1) * seq_len] for s in range(n_seq)], axis=0)
    n_rows = n_seq * N_EXPERTS
    capf = float(cap)

    def count(mask):
        return jnp.sum(jnp.where(mask, 1.0, 0.0), axis=1, keepdims=True)

    def as_float(bits):
        return lax.bitcast_convert_type(bits, F32)

    thr = jnp.zeros((n_rows, 1), jnp.int32)
    for bit in range(30, -1, -1):
        cand = thr | (1 << bit)
        thr = jnp.where(count(a >= as_float(cand)) >= capf, cand, thr)
    above = a >= as_float(thr + 1)
    tie = jnp.logical_and(a >= as_float(thr), jnp.logical_not(above))
    need = capf - count(above)

    idx = lax.broadcasted_iota(jnp.int32, (n_rows, seq_len), 1)
    last = jnp.zeros((n_rows, 1), jnp.int32)
    for bit in range(seq_len.bit_length() - 2, -1, -1):
        cand = last | (1 << bit)
        before = jnp.sum(jnp.where(tie, jnp.where(idx < cand, 1.0, 0.0), 0.0), axis=1, keepdims=True)
        last = jnp.where(before < need, cand, last)
    keep = jnp.where(above, 1.0, jnp.where(tie, jnp.where(idx <= last, 1.0, 0.0), 0.0))

    blk = tri_ref.shape[0]
    offset = jnp.zeros((n_rows, 1), F32)
    pos = []
    for c in range(seq_len // blk):
        kb = keep[:, c * blk:(c + 1) * blk]
        pos.append(_dot(kb.astype(BF16), tri_ref[...]) + offset)
        offset = offset + jnp.sum(kb, axis=1, keepdims=True)
    pos = jnp.concatenate(pos, axis=1)
    pos_ref[...] = jnp.where(keep > 0.0, pos, -1.0)
    gate_ref[...] = a


def _route(aff_t, tri, n_seq, seq_len, cap, col0):
    n_rows = n_seq * N_EXPERTS
    width = n_seq * seq_len
    shape = jax.ShapeDtypeStruct((n_rows, seq_len), F32)
    spec = pl.BlockSpec((n_rows, seq_len), lambda i: (0, 0))
    return pl.pallas_call(
        functools.partial(_route_body, n_seq=n_seq, seq_len=seq_len, cap=cap),
        grid=(1,),
        in_specs=[
            pl.BlockSpec((N_EXPERTS, width), lambda i: (0, col0 // width)),
            pl.BlockSpec(tri.shape, lambda i: (0, 0)),
        ],
        out_specs=[spec, spec],
        out_shape=[shape, shape],
        compiler_params=_cparams(1, 32),
        name="route_ctx" if cap == CAP_CTX else "route_lat",
    )(aff_t, tri)


def _slot_onehot(pos_ref, cap, values_ref=None):
    n_tok = pos_ref.shape[1]
    slot = lax.broadcasted_iota(jnp.int32, (cap, n_tok), 0).astype(F32)
    rows = []
    for e in range(N_EXPERTS):
        hit = pos_ref[e:e + 1, :] == slot
        val = 1.0 if values_ref is None else values_ref[e:e + 1, :]
        rows.append(jnp.where(hit, val, 0.0).astype(BF16))
    return jnp.concatenate(rows, axis=0)


def _gather_ctx_body(pos_ref, h_ref, o_ref):
    onehot = _slot_onehot(pos_ref, CAP_CTX)
    o_ref[...] = _dot(onehot, h_ref[...]).astype(BF16).reshape(N_EXPERTS, CAP_CTX, D_MODEL)


def _gather_ctx(pos, h2):
    return pl.pallas_call(
        _gather_ctx_body,
        grid=(BATCH,),
        in_specs=[
            pl.BlockSpec((N_EXPERTS, SEQ), lambda s: (s, 0)),
            pl.BlockSpec((SEQ, D_MODEL), lambda s: (s, 0)),
        ],
        out_specs=pl.BlockSpec((N_EXPERTS, CAP_CTX, D_MODEL), lambda s: (0, s, 0)),
        out_shape=jax.ShapeDtypeStruct((N_EXPERTS, BATCH * CAP_CTX, D_MODEL), BF16),
        compiler_params=_cparams(1, 32),
        name="gather_ctx",
    )(pos, h2)


def _gather_lat_body(pos_ref, h_ref, o_ref):
    e = pl.program_id(1)
    slot = lax.broadcasted_iota(jnp.int32, (CAP_LAT, DEC_SEQ), 0).astype(F32)
    onehot = jnp.where(pos_ref[pl.ds(e, 1), :] == slot, 1.0, 0.0).astype(BF16)
    o_ref[...] = _dot(onehot, h_ref[...]).astype(BF16)


def _gather_lat(pos, h2):
    blk0 = N_CTX // DEC_SEQ
    return pl.pallas_call(
        _gather_lat_body,
        grid=(DEC_BATCH, N_EXPERTS),
        in_specs=[
            pl.BlockSpec((N_EXPERTS, DEC_SEQ), lambda b, e: (b, 0)),
            pl.BlockSpec((DEC_SEQ, D_MODEL), lambda b, e: (blk0 + b, 0)),
        ],
        out_specs=pl.BlockSpec((None, CAP_LAT, D_MODEL), lambda b, e: (e, b, 0)),
        out_shape=jax.ShapeDtypeStruct((N_EXPERTS, DEC_BATCH * CAP_LAT, D_MODEL), BF16),
        compiler_params=_cparams(2, 40),
        name="gather_lat",
    )(pos, h2)


def _ffn_body(xc_ref, xl_ref, w1_ref, w3_ref, w2_ref, yc_ref, yl_ref):
    w1 = w1_ref[...].astype(BF16)
    w3 = w3_ref[...].astype(BF16)
    w2 = w2_ref[...].astype(BF16)
    rows = 512
    for x_ref, y_ref in ((xc_ref, yc_ref), (xl_ref, yl_ref)):
        for r0 in range(0, x_ref.shape[0], rows):
            x = x_ref[r0:r0 + rows, :]
            a = _dot(x, w1)
            act = (a * _sigmoid(a) * _dot(x, w3)).astype(BF16)
            y_ref[r0:r0 + rows, :] = _dot(act, w2).astype(BF16)


def _ffn(xg_ctx, xg_lat, w1, w3, w2, l):
    n_c, n_l = xg_ctx.shape[1], xg_lat.shape[1]

    def wspec():
        return pl.BlockSpec((None, None, D_MODEL, D_MODEL), lambda e: (l, e, 0, 0))

    def xspec(n):
        return pl.BlockSpec((None, n, D_MODEL), lambda e: (e, 0, 0))

    return pl.pallas_call(
        _ffn_body,
        grid=(N_EXPERTS,),
        in_specs=[xspec(n_c), xspec(n_l), wspec(), wspec(), wspec()],
        out_specs=[xspec(n_c), xspec(n_l)],
        out_shape=[jax.ShapeDtypeStruct(xg_ctx.shape, BF16), jax.ShapeDtypeStruct(xg_lat.shape, BF16)],
        compiler_params=_cparams(1, 56),
        name="ffn",
    )(xg_ctx, xg_lat, w1, w3, w2)


def _combine_body(pos_ref, gate_ref, y_ref, x_ref, mod_ref, o_ref, *, cap, seg0):
    weights = _slot_onehot(pos_ref, cap, gate_ref)
    y = y_ref[...].reshape(N_EXPERTS * cap, D_MODEL)
    seg = seg0 + pl.program_id(0) if seg0 else 0
    o_ref[...] = x_ref[...] + _mod_row(mod_ref, seg, 5) * _dot_tn(weights, y)


def _combine_ctx(pos, gate, y, x_mid, mod, l):
    return pl.pallas_call(
        functools.partial(_combine_body, cap=CAP_CTX, seg0=0),
        grid=(BATCH,),
        in_specs=[
            pl.BlockSpec((N_EXPERTS, SEQ), lambda s: (s, 0)),
            pl.BlockSpec((N_EXPERTS, SEQ), lambda s: (s, 0)),
            pl.BlockSpec((N_EXPERTS, CAP_CTX, D_MODEL), lambda s: (0, s, 0)),
            pl.BlockSpec((SEQ, D_MODEL), lambda s: (s, 0)),
            pl.BlockSpec((None, 8, 6 * D_MODEL), lambda s: (l, 0, 0)),
        ],
        out_specs=pl.BlockSpec((SEQ, D_MODEL), lambda s: (s, 0)),
        out_shape=jax.ShapeDtypeStruct((N_TOK, D_MODEL), F32),
        input_output_aliases={3: 0},
        compiler_params=_cparams(1, 32),
        name="combine_ctx",
    )(pos, gate, y, x_mid, mod)


def _combine_lat(pos, gate, y, x_mid, mod, l):
    tiles = DEC_SEQ // CHUNK
    blk0 = N_CTX // CHUNK
    return pl.pallas_call(
        functools.partial(_combine_body, cap=CAP_LAT, seg0=1),
        grid=(DEC_BATCH, tiles),
        in_specs=[
            pl.BlockSpec((N_EXPERTS, CHUNK), lambda b, t: (b, t)),
            pl.BlockSpec((N_EXPERTS, CHUNK), lambda b, t: (b, t)),
            pl.BlockSpec((N_EXPERTS, CAP_LAT, D_MODEL), lambda b, t: (0, b, 0)),
            pl.BlockSpec((CHUNK, D_MODEL), lambda b, t: (blk0 + b * tiles + t, 0)),
            pl.BlockSpec((None, 8, 6 * D_MODEL), lambda b, t: (l, 0, 0)),
        ],
        out_specs=pl.BlockSpec((CHUNK, D_MODEL), lambda b, t: (blk0 + b * tiles + t, 0)),
        out_shape=jax.ShapeDtypeStruct((N_TOK, D_MODEL), F32),
        input_output_aliases={3: 0},
        compiler_params=_cparams(2, 48),
        name="combine_lat",
    )(pos, gate, y, x_mid, mod)


def _final_norm_body(x_ref, g_ref, o_ref):
    x = x_ref[...]
    o_ref[...] = x * lax.rsqrt(jnp.mean(x * x, axis=-1, keepdims=True) + EPS) * g_ref[...]


def _final_norm(x, g, row0, n_rows):
    tm = 512
    return pl.pallas_call(
        _final_norm_body,
        grid=(n_rows // tm,),
        in_specs=[
            pl.BlockSpec((tm, D_MODEL), lambda i: (row0 // tm + i, 0)),
            pl.BlockSpec((1, D_MODEL), lambda i: (0, 0)),
        ],
        out_specs=pl.BlockSpec((tm, D_MODEL), lambda i: (i, 0)),
        out_shape=jax.ShapeDtypeStruct((n_rows, D_MODEL), F32),
        compiler_params=_cparams(1, 32),
        name="final_norm",
    )(x, g)


def kernel(x_prompt, x_sample, cache_k, cache_v, c, c_ctx, w_ada, b_ada, norm1_g, w_in, rpb, conv_w, conv_b,
           conv_ln_g, conv_ln_b, pool_w, pool_scale, sgu_ln_g, sgu_ln_b, sgu_w, sgu_b, w_branch, w_out,
           norm2_g, w_router, w1, w3, w2, final_g):
    def vecs(a):
        return a.reshape(DEPTH, 1, a.shape[-1])

    cond8 = jnp.concatenate([c_ctx[None, :], c, jnp.zeros((8 - 1 - DEC_BATCH, D_MODEL), F32)], axis=0)
    mod = _adaln(cond8, w_ada, b_ada)
    table = _rpb_table(rpb)
    cos_np, sin_np = _rope_tables()
    cos, sin = jnp.asarray(cos_np), jnp.asarray(sin_np)
    tri = jnp.asarray(np.triu(np.ones((CHUNK, CHUNK), np.float32), 1), BF16)

    w_in_bf = w_in.astype(BF16)
    w_branch_bf = w_branch.astype(BF16)
    w_out_bf = w_out.astype(BF16)
    pool_w_bf = pool_w.astype(BF16)
    sgu_w_bf = sgu_w.astype(BF16)
    sgu_bt = sgu_b.transpose(0, 2, 1)
    w_router_pad = jnp.pad(w_router, ((0, 0), (0, 0), (0, LANES - N_EXPERTS)))
    w_router_hi = w_router_pad.astype(BF16)
    w_router_lo = (w_router_pad - w_router_hi.astype(F32)).astype(BF16)
    norm1_v, norm2_v = vecs(norm1_g), vecs(norm2_g)
    conv_b_v, conv_ln_g_v, conv_ln_b_v = vecs(conv_b), vecs(conv_ln_g), vecs(conv_ln_b)
    pool_scale_v, sgu_ln_g_v, sgu_ln_b_v = vecs(pool_scale), vecs(sgu_ln_g), vecs(sgu_ln_b)

    x = jnp.concatenate([x_prompt.reshape(N_CTX, D_MODEL), x_sample.reshape(N_LAT, D_MODEL)], axis=0)
    projs = []
    for l in range(DEPTH):
        proj = _proj(x, norm1_v, mod, w_in_bf, l)
        projs.append(proj)
        att_ctx = _attn_ctx(proj)
        att_lat = _attn_lat(proj, cache_k, cache_v, table, cos, sin, l)
        o_conv, o_pool, o_sgu = _seqmix(proj, conv_w, conv_b_v, conv_ln_g_v, conv_ln_b_v, pool_w_bf,
                                        pool_scale_v, sgu_ln_g_v, sgu_ln_b_v, sgu_w_bf, sgu_bt, l)
        x_mid, h2, aff_t = _merge(x, att_ctx, att_lat, o_conv, o_pool, o_sgu, mod, norm1_v, norm2_v,
                                  w_in_bf, w_branch_bf, w_out_bf, w_router_hi, w_router_lo, l)
        pos_c, gate_c = _route(aff_t, tri, BATCH, SEQ, CAP_CTX, 0)
        pos_l, gate_l = _route(aff_t, tri, DEC_BATCH, DEC_SEQ, CAP_LAT, N_CTX)
        y_c, y_l = _ffn(_gather_ctx(pos_c, h2), _gather_lat(pos_l, h2), w1, w3, w2, l)
        x = _combine_ctx(pos_c, gate_c, y_c, x_mid, mod, l)
        x = _combine_lat(pos_l, gate_l, y_l, x, mod, l)

    y_prompt = _final_norm(x, final_g[None, :], 0, N_CTX).reshape(BATCH, SEQ, D_MODEL)
    y_sample = _final_norm(x, final_g[None, :], N_CTX, N_LAT).reshape(DEC_BATCH, DEC_SEQ, D_MODEL)
    new_k, new_v = _kv_out(projs)
    return (y_prompt, y_sample, new_k, new_v)
```

```python
import functools

import numpy as np
import jax
import jax.numpy as jnp
from jax import lax
from jax.experimental import pallas as pl
from jax.experimental.pallas import tpu as pltpu

F32 = jnp.float32
BF16 = jnp.bfloat16

D_MODEL = 1024
BATCH = 32
SEQ = 256
DEPTH = 4
DEC_BATCH = 2
DEC_SEQ = 2048
PAST_LEN = 512
GRID_W = 64
GRID_ROWS = DEC_SEQ // GRID_W
N_HEADS = 8
HEAD_DIM = 64
W_ATT = N_HEADS * HEAD_DIM
NA_ROWS = 8
NA_COLS = 16
ROPE_BASE = 10000.0
W_CONV = 512
CONV_K = 31
W_POOL = 512
POOL_WINDOWS = (2, 4, 8, 16)
POOL_GROUP = 128
W_SGU = 512
SGU_GROUPS = 4
SGU_CHUNK = 128
N_BRANCH = 4
BRANCH_W = 512
N_EXPERTS = 16
EC_CAPACITY = 2
EPS = 1e-6
NEG_INF = -1e30

N_CTX = BATCH * SEQ
N_LAT = DEC_BATCH * DEC_SEQ
N_TOK = N_CTX + N_LAT
D_PROJ = 4096
COL_BLK = 512
CAP_CTX = EC_CAPACITY * SEQ // N_EXPERTS
CAP_LAT = EC_CAPACITY * DEC_SEQ // N_EXPERTS
N_ROW_OFF = 2 * NA_ROWS
Q_ROWS = 4
KEY_ROWS = 12
KV_SEQS = 4
HEAD_GROUP = 4
HALO = 16
CHUNK = 256
LANES = 128
SUBLANES = 8
VMEM_LIMIT_CAP = 56 * 1024 * 1024

HIGHEST = lax.Precision.HIGHEST


def _cparams(n_grid, vmem_mb):
    return pltpu.CompilerParams(
        dimension_semantics=("arbitrary",) * n_grid,
        vmem_limit_bytes=min(vmem_mb * 1024 * 1024, VMEM_LIMIT_CAP))


def _dot(a, b):
    return jnp.dot(a, b, preferred_element_type=F32)


def _dot_tn(a, b):
    return lax.dot_general(a, b, (((0,), (0,)), ((), ())), preferred_element_type=F32)


def _sigmoid(x):
    return 0.5 * jnp.tanh(0.5 * x) + 0.5


def _segment_of_row(row0):
    return jnp.where(row0 < N_CTX, 0, 1 + (row0 - N_CTX) // DEC_SEQ)


def _mod_row(mod_ref, seg, k):
    return mod_ref[pl.ds(seg, 1), k * D_MODEL:(k + 1) * D_MODEL]


def _norm_modulate(x, g, shift, scale):
    y = x * lax.rsqrt(jnp.mean(x * x, axis=-1, keepdims=True) + EPS) * g
    return y * (1.0 + scale) + shift


def _layernorm(x, g, b):
    mu = jnp.mean(x, axis=-1, keepdims=True)
    xc = x - mu
    var = jnp.mean(xc * xc, axis=-1, keepdims=True)
    return xc * lax.rsqrt(var + EPS) * g + b


def _adaln_body(cond_ref, w_ref, b_ref, o_ref):
    c = cond_ref[...]
    a = (c * _sigmoid(c)).astype(BF16)
    o_ref[...] = _dot(a, w_ref[...].astype(BF16)) + b_ref[...]


def _adaln(cond8, w_ada, b_ada):
    tn = 1536
    n6 = 6 * D_MODEL
    return pl.pallas_call(
        _adaln_body,
        grid=(DEPTH, n6 // tn),
        in_specs=[
            pl.BlockSpec((8, D_MODEL), lambda l, j: (0, 0)),
            pl.BlockSpec((None, D_MODEL, tn), lambda l, j: (l, 0, j)),
            pl.BlockSpec((None, 1, tn), lambda l, j: (l, 0, j)),
        ],
        out_specs=pl.BlockSpec((None, 8, tn), lambda l, j: (l, 0, j)),
        out_shape=jax.ShapeDtypeStruct((DEPTH, 8, n6), F32),
        compiler_params=_cparams(2, 32),
        name="adaln",
    )(cond8, w_ada, b_ada.reshape(DEPTH, 1, n6))


def _rpb_onehots():
    cq = np.arange(GRID_W)[None, :]
    kc = np.arange(GRID_W)[:, None]
    start = np.clip(cq - NA_COLS // 2, 0, GRID_W - NA_COLS)
    valid = (kc >= start) & (kc < start + NA_COLS)
    d_col = np.clip(kc - cq, -(NA_COLS - 1), NA_COLS - 1) + (NA_COLS - 1)
    left = np.zeros((32, GRID_W, 2 * GRID_W), np.float32)
    right = np.zeros((32, GRID_W, 2 * GRID_W), np.float32)
    mask = np.zeros((GRID_W, 2 * GRID_W), np.float32)
    for j in range(2 * NA_COLS - 1):
        hit = (d_col == j) & valid
        left[j, :, :GRID_W] = hit
        right[j, :, GRID_W:] = hit
    mask[:, :GRID_W] = np.where(valid, 0.0, NEG_INF)
    mask[:, GRID_W:] = np.where(valid, 0.0, NEG_INF)
    n = GRID_W * 2 * GRID_W
    return left.reshape(32, n), right.reshape(32, n), mask.reshape(1, n)


def _rpb_body(a_ref, b_ref, left_ref, right_ref, mask_ref, o_ref):
    t = jnp.dot(a_ref[...], left_ref[...], precision=HIGHEST, preferred_element_type=F32)
    t = t + jnp.dot(b_ref[...], right_ref[...], precision=HIGHEST, preferred_element_type=F32)
    o_ref[...] = t + mask_ref[...]


def _rpb_table(rpb):
    left, right, mask = _rpb_onehots()
    n_rows = N_HEADS * N_ROW_OFF
    rpb_p = jnp.pad(rpb, ((0, 0), (0, 0), (0, 0), (0, 1)))
    a = jnp.pad(rpb_p, ((0, 0), (0, 0), (0, 1), (0, 0))).reshape(DEPTH, n_rows, 32)
    b = jnp.pad(rpb_p, ((0, 0), (0, 0), (1, 0), (0, 0))).reshape(DEPTH, n_rows, 32)
    n = left.shape[1]
    out = pl.pallas_call(
        _rpb_body,
        grid=(DEPTH,),
        in_specs=[
            pl.BlockSpec((None, n_rows, 32), lambda l: (l, 0, 0)),
            pl.BlockSpec((None, n_rows, 32), lambda l: (l, 0, 0)),
            pl.BlockSpec((32, n), lambda l: (0, 0)),
            pl.BlockSpec((32, n), lambda l: (0, 0)),
            pl.BlockSpec((1, n), lambda l: (0, 0)),
        ],
        out_specs=pl.BlockSpec((None, n_rows, n), lambda l: (l, 0, 0)),
        out_shape=jax.ShapeDtypeStruct((DEPTH, n_rows, n), F32),
        compiler_params=_cparams(1, 32),
        name="rpb_table",
    )(a, b, jnp.asarray(left), jnp.asarray(right), jnp.asarray(mask))
    return out.reshape(DEPTH, N_HEADS, N_ROW_OFF, GRID_W, 2 * GRID_W)


def _proj_body(x_ref, g_ref, mod_ref, w_ref, o_ref, h_scr, *, tm):
    i = pl.program_id(0)

    @pl.when(pl.program_id(1) == 0)
    def _():
        seg = _segment_of_row(i * tm)
        h = _norm_modulate(x_ref[...], g_ref[...], _mod_row(mod_ref, seg, 0), _mod_row(mod_ref, seg, 1))
        h_scr[...] = h.astype(BF16)

    o_ref[...] = _dot(h_scr[...], w_ref[...]).astype(BF16)


def _proj(x, norm_g, mod, w_in_bf, l):
    tm, tn = 1024, 1024
    return pl.pallas_call(
        functools.partial(_proj_body, tm=tm),
        grid=(N_TOK // tm, D_PROJ // tn),
        in_specs=[
            pl.BlockSpec((tm, D_MODEL), lambda i, j: (i, 0)),
            pl.BlockSpec((None, 1, D_MODEL), lambda i, j: (l, 0, 0)),
            pl.BlockSpec((None, 8, 6 * D_MODEL), lambda i, j: (l, 0, 0)),
            pl.BlockSpec((None, D_MODEL, tn), lambda i, j: (l, 0, j)),
        ],
        out_specs=pl.BlockSpec((tm, tn), lambda i, j: (i, j)),
        out_shape=jax.ShapeDtypeStruct((N_TOK, D_PROJ), BF16),
        scratch_shapes=[pltpu.VMEM((tm, D_MODEL), BF16)],
        compiler_params=_cparams(2, 40),
        name="proj",
    )(x, norm_g, mod, w_in_bf)


def _head_rows_in_pair(x_t, h):
    rows = x_t[h * HEAD_DIM:(h + 1) * HEAD_DIM, :]
    zeros = jnp.zeros_like(rows)
    return jnp.concatenate([rows, zeros] if h % 2 == 0 else [zeros, rows], axis=0)


def _pair_lanes(h):
    return slice((h // 2) * LANES, (h // 2 + 1) * LANES)


def _attn_ctx_body(q_ref, k_ref, v_ref, o_ref):
    k = k_ref[...]
    v = v_ref[...]
    q_t = (q_ref[...].astype(F32) * (HEAD_DIM ** -0.5)).T.astype(BF16)
    v_t = v.astype(F32).T.astype(BF16)
    heads = range(N_HEADS)
    s_t = [_dot(k[:, _pair_lanes(h)], _head_rows_in_pair(q_t, h)) for h in heads]
    e_t = [jnp.exp(s - jnp.max(s, axis=0, keepdims=True)) for s in s_t]
    denom = [jnp.sum(e, axis=0, keepdims=True) for e in e_t]
    o_t = [_dot(v_t[h * HEAD_DIM:(h + 1) * HEAD_DIM, :], e_t[h].astype(BF16)) / denom[h] for h in heads]
    o_ref[...] = jnp.concatenate(o_t, axis=0).T.astype(BF16)


def _attn_ctx(proj):
    return pl.pallas_call(
        _attn_ctx_body,
        grid=(BATCH,),
        in_specs=[
            pl.BlockSpec((SEQ, COL_BLK), lambda b: (b, 0)),
            pl.BlockSpec((SEQ, COL_BLK), lambda b: (b, 1)),
            pl.BlockSpec((SEQ, COL_BLK), lambda b: (b, 2)),
        ],
        out_specs=pl.BlockSpec((SEQ, W_ATT), lambda b: (b, 0)),
        out_shape=jax.ShapeDtypeStruct((N_CTX, W_ATT), BF16),
        compiler_params=_cparams(1, 32),
        name="attn_ctx",
    )(proj, proj, proj)


def _kv_out_body(*refs):
    kv_refs, (ck_ref, cv_ref) = refs[:2 * DEPTH], refs[2 * DEPTH:]
    l = pl.program_id(0)
    for j in range(DEPTH):
        @pl.when(l == j)
        def _(j=j):
            for s in range(KV_SEQS):
                rows = slice(s * SEQ, (s + 1) * SEQ)
                k = kv_refs[2 * j][rows, :]
                v = kv_refs[2 * j + 1][rows, :]
                for h in range(N_HEADS):
                    sl = slice(h * HEAD_DIM, (h + 1) * HEAD_DIM)
                    ck_ref[s, h] = k[:, sl].astype(F32)
                    cv_ref[s, h] = v[:, sl].astype(F32)


def _kv_out(projs):
    steps = BATCH // KV_SEQS

    def src(j, col):
        return pl.BlockSpec((KV_SEQS * SEQ, COL_BLK),
                            lambda l, b: (jnp.where(l == j, b, jnp.where(l < j, 0, steps - 1)), col))

    cache_spec = pl.BlockSpec((KV_SEQS, None, N_HEADS, SEQ, HEAD_DIM), lambda l, b: (b, l, 0, 0, 0))
    cache_shape = jax.ShapeDtypeStruct((BATCH, DEPTH, N_HEADS, SEQ, HEAD_DIM), F32)
    in_specs, args = [], []
    for j in range(DEPTH):
        in_specs += [src(j, 1), src(j, 2)]
        args += [projs[j], projs[j]]
    return pl.pallas_call(
        _kv_out_body,
        grid=(DEPTH, steps),
        in_specs=in_specs,
        out_specs=[cache_spec, cache_spec],
        out_shape=[cache_shape, cache_shape],
        compiler_params=_cparams(2, 48),
        name="kv_out",
    )(*args)


def _rope_tables():
    t = np.arange(DEC_SEQ)
    half = HEAD_DIM // 2
    nf = half // 2
    inv = (1.0 / (ROPE_BASE ** (np.arange(nf) / nf))).astype(np.float32)
    cos = np.zeros((DEC_SEQ, HEAD_DIM), np.float32)
    sin = np.zeros((DEC_SEQ, HEAD_DIM), np.float32)
    for blk, pos in enumerate((t // GRID_W, t % GRID_W)):
        ang = (pos[:, None].astype(np.float32) * inv[None, :]).astype(np.float32)
        c, s = np.cos(ang), np.sin(ang)
        cos[:, blk * half:(blk + 1) * half] = np.concatenate([c, c], axis=1)
        sin[:, blk * half:(blk + 1) * half] = np.concatenate([-s, s], axis=1)
    return np.tile(cos, (1, 2)), np.tile(sin, (1, 2))


def _rope(x, cos, sin):
    cos = jnp.concatenate([cos] * (W_ATT // LANES), axis=1)
    sin = jnp.concatenate([sin] * (W_ATT // LANES), axis=1)
    lane = lax.broadcasted_iota(jnp.int32, x.shape, 1)
    nf = HEAD_DIM // 4
    partner = jnp.where(lane % (2 * nf) < nf, pltpu.roll(x, W_ATT - nf, 1), pltpu.roll(x, nf, 1))
    return x * cos + partner * sin


def _attn_lat_body(q_ref, k_ref, v_ref, ck_ref, cv_ref, tab_ref, cos_ref, sin_ref, o_ref,
                   krot_scr, vt_scr, ck_scr, cvt_scr):
    g = pl.program_id(1)
    blk = Q_ROWS * GRID_W
    n_blk = DEC_SEQ // blk

    @pl.when(g == 0)
    def _():
        def chunk(c, carry):
            rows = pl.ds(pl.multiple_of(c * blk, blk), blk)
            kr = _rope(k_ref[rows, :].astype(F32), cos_ref[rows, :], sin_ref[rows, :])
            krot_scr[rows, :] = kr.astype(BF16)
            vt_scr[c] = v_ref[rows, :].astype(F32).T.astype(BF16)
            return carry
        lax.fori_loop(0, n_blk, chunk, 0)
        for h in range(0, N_HEADS, 2):
            ck_scr[h] = ck_ref[h].astype(BF16)
            ck_scr[h + 1] = ck_ref[h + 1].astype(BF16)
            pair_t = jnp.concatenate([cv_ref[h], cv_ref[h + 1]], axis=1).T
            cvt_scr[h] = pair_t[:HEAD_DIM, :].astype(BF16)
            cvt_scr[h + 1] = pair_t[HEAD_DIM:, :].astype(BF16)

    kh = min(NA_ROWS, GRID_ROWS)
    scale = HEAD_DIM ** -0.5
    r0 = g * Q_ROWS
    blk0 = jnp.clip(g - 1, 0, n_blk - KEY_ROWS // Q_ROWS)
    u = blk0 * Q_ROWS

    qrows = pl.ds(pl.multiple_of(g * blk, blk), blk)
    q = q_ref[...].astype(F32)
    q_rot_t = (_rope(q, cos_ref[qrows, :], sin_ref[qrows, :]) * scale).T.astype(BF16)
    q_plain_t = (q * scale).T.astype(BF16)
    k_win = krot_scr[pl.ds(pl.multiple_of(blk0 * blk, blk), KEY_ROWS * GRID_W), :]

    lane = lax.broadcasted_iota(jnp.int32, (GRID_W, 2 * GRID_W), 1)
    row_masks = []
    for kr in range(KEY_ROWS):
        pair_masks = []
        for ip in range(Q_ROWS // 2):
            m = []
            for i in (2 * ip, 2 * ip + 1):
                start = jnp.clip(r0 + i - kh // 2, 0, GRID_ROWS - kh)
                inside = jnp.logical_and(u + kr >= start, u + kr < start + kh)
                m.append(jnp.where(inside, 0.0, NEG_INF))
            pair_masks.append(jnp.where(lane < GRID_W, m[0], m[1]))
        row_masks.append(jnp.concatenate(pair_masks, axis=1))

    def head_rows(h):
        return slice(h * HEAD_DIM, (h + 1) * HEAD_DIM)

    def local_bias(h):
        bias_rows = []
        for kr in range(KEY_ROWS):
            tiles = []
            for ip in range(Q_ROWS // 2):
                d_row = u + kr - (r0 + 2 * ip) + (NA_ROWS - 1)
                tiles.append(tab_ref[h, jnp.clip(d_row, 0, N_ROW_OFF - 1)])
            bias_rows.append(jnp.concatenate(tiles, axis=1) + row_masks[kr])
        return jnp.concatenate(bias_rows, axis=0)

    outs = []
    for h0 in range(0, N_HEADS, HEAD_GROUP):
        heads = range(h0, h0 + HEAD_GROUP)
        s_loc = [_dot(k_win[:, _pair_lanes(h)], _head_rows_in_pair(q_rot_t, h)) + local_bias(h)
                 for h in heads]
        s_ctx = [_dot(ck_scr[h], q_plain_t[head_rows(h), :]) for h in heads]
        m = [jnp.maximum(jnp.max(a, axis=0, keepdims=True), jnp.max(b, axis=0, keepdims=True))
             for a, b in zip(s_loc, s_ctx)]
        e_loc = [jnp.exp(a - mm) for a, mm in zip(s_loc, m)]
        e_ctx = [jnp.exp(b - mm) for b, mm in zip(s_ctx, m)]
        denom = [jnp.sum(a, axis=0, keepdims=True) + jnp.sum(b, axis=0, keepdims=True)
                 for a, b in zip(e_loc, e_ctx)]
        for i, h in enumerate(heads):
            p_loc = e_loc[i].astype(BF16)
            o_t = _dot(cvt_scr[h], e_ctx[i].astype(BF16))
            for c in range(KEY_ROWS // Q_ROWS):
                o_t = o_t + _dot(vt_scr[blk0 + c, head_rows(h), :], p_loc[c * blk:(c + 1) * blk, :])
            outs.append(o_t / denom[i])
    o_ref[...] = jnp.concatenate(outs, axis=0).T.astype(BF16)


def _attn_lat(proj, cache_k, cache_v, table, cos, sin, l):
    blk = Q_ROWS * GRID_W
    steps = GRID_ROWS // Q_ROWS
    q_blk0 = N_CTX // blk
    kv_blk0 = N_CTX // DEC_SEQ
    cache_spec = pl.BlockSpec((None, None, N_HEADS, PAST_LEN, HEAD_DIM), lambda b, g: (b, l, 0, 0, 0))

    def const(shape, index_map):
        return pl.BlockSpec(shape, index_map, pipeline_mode=pl.Buffered(1))

    return pl.pallas_call(
        _attn_lat_body,
        grid=(DEC_BATCH, steps),
        in_specs=[
            pl.BlockSpec((blk, COL_BLK), lambda b, g: (q_blk0 + b * steps + g, 0)),
            pl.BlockSpec((DEC_SEQ, COL_BLK), lambda b, g: (kv_blk0 + b, 1)),
            pl.BlockSpec((DEC_SEQ, COL_BLK), lambda b, g: (kv_blk0 + b, 2)),
            cache_spec, cache_spec,
            const((None, N_HEADS, N_ROW_OFF, GRID_W, 2 * GRID_W), lambda b, g: (l, 0, 0, 0, 0)),
            const((DEC_SEQ, LANES), lambda b, g: (0, 0)),
            const((DEC_SEQ, LANES), lambda b, g: (0, 0)),
        ],
        out_specs=pl.BlockSpec((blk, W_ATT), lambda b, g: (b * steps + g, 0)),
        out_shape=jax.ShapeDtypeStruct((N_LAT, W_ATT), BF16),
        scratch_shapes=[
            pltpu.VMEM((DEC_SEQ, W_ATT), BF16),
            pltpu.VMEM((DEC_SEQ // blk, W_ATT, blk), BF16),
            pltpu.VMEM((N_HEADS, PAST_LEN, HEAD_DIM), BF16),
            pltpu.VMEM((N_HEADS, HEAD_DIM, PAST_LEN), BF16),
        ],
        compiler_params=_cparams(2, 48),
        name="attn_lat",
    )(proj, proj, proj, cache_k, cache_v, table, cos, sin)


def _interleave(major, minor):
    per = -(-len(minor) // len(major))
    for n, step in enumerate(major):
        step()
        for other in minor[n * per:(n + 1) * per]:
            other()


def _mixers_steps(chunk, src, mix, work, dst):
    cw_ref, cb_ref, clg_ref, clb_ref, pw_ref, ps_ref, slg_ref, slb_ref, sw_ref, sbt_ref = mix
    ypad, zpad, yshift, acc_scr, ug_scr, vn_scr = work
    n_ctx_chunks = N_CTX // CHUNK
    per_seq = DEC_SEQ // CHUNK
    j = (chunk - n_ctx_chunks) % per_seq
    is_lat = chunk >= n_ctx_chunks
    left_ok = jnp.logical_and(is_lat, j > 0)
    right_ok = jnp.logical_and(is_lat, j < per_seq - 1)
    seq_len = jnp.where(is_lat, DEC_SEQ, SEQ)
    chunk_off = jnp.where(is_lat, j * CHUNK, 0)

    def glu(a_, g_):
        return a_.astype(F32) * _sigmoid(g_.astype(F32))

    def fill():
        prev, nxt = src["prev"](), src["nxt"]()
        ypad[0:HALO, :] = jnp.where(left_ok, glu(prev[0], prev[1]), 0.0)
        ypad[HALO:HALO + CHUNK, :] = glu(src["a"](), src["g"]())
        ypad[HALO + CHUNK:, :] = jnp.where(right_ok, glu(nxt[0], nxt[1]), 0.0)
        zpad[0:HALO, :] = jnp.where(left_ok, prev[2].astype(F32), 0.0)
        zpad[HALO:HALO + CHUNK, :] = src["p"]().astype(F32)
        zpad[HALO + CHUNK:, :] = jnp.where(right_ok, nxt[2].astype(F32), 0.0)

    def shift(s):
        def run():
            yshift[s] = ypad[s:s + yshift.shape[1], :]
        return run

    conv_rows = 64
    half_k = CONV_K // 2

    def conv(rb, cg):
        def run():
            lanes = slice(cg * LANES, (cg + 1) * LANES)
            acc = jnp.zeros((conv_rows, LANES), F32)
            for k in range(CONV_K):
                start = HALO + rb * conv_rows + k - half_k
                s = start % SUBLANES
                acc = acc + cw_ref[k:k + 1, lanes] * yshift[s, start - s:start - s + conv_rows, lanes]
            acc_scr[rb * conv_rows:(rb + 1) * conv_rows, lanes] = acc
        return run

    def conv_finish():
        y = _layernorm(acc_scr[...] + cb_ref[...], clg_ref[...], clb_ref[...])
        dst[0] = (y * _sigmoid(y)).astype(BF16)

    def pool(gi, w):
        def run():
            t_seq = lax.broadcasted_iota(jnp.int32, (CHUNK, 1), 0) + chunk_off
            lanes = slice(gi * POOL_GROUP, (gi + 1) * POOL_GROUP)
            lo, hi = -(w // 2), w - w // 2 - 1
            tot = jnp.zeros((CHUNK, POOL_GROUP), F32)
            for jj in range(lo, hi + 1):
                tot = tot + zpad[HALO + jj:HALO + jj + CHUNK, lanes]
            cnt = w - jnp.maximum(0, -lo - t_seq) - jnp.maximum(0, t_seq + hi - (seq_len - 1))
            d = tot / cnt.astype(F32) - zpad[HALO:HALO + CHUNK, lanes]
            dst[1, :, lanes] = (_dot(d.astype(BF16), pw_ref[gi]) * ps_ref[:, lanes]).astype(BF16)
        return run

    def gating_inputs():
        ug_scr[...] = jax.nn.gelu(src["u"]().astype(F32))
        vn_scr[...] = _layernorm(jax.nn.gelu(src["v"]().astype(F32)), slg_ref[...], slb_ref[...]).astype(BF16)

    def gating(n):
        def run():
            pos = slice(n * SGU_CHUNK, (n + 1) * SGU_CHUNK)
            for grp in range(SGU_GROUPS):
                lanes = slice(grp * LANES, (grp + 1) * LANES)
                mixed = _dot(sw_ref[grp], vn_scr[pos, lanes]) + sbt_ref[:, grp:grp + 1]
                dst[2, pos, lanes] = (ug_scr[pos, lanes] * mixed).astype(BF16)
        return run

    steps = [fill] + [shift(s) for s in range(SUBLANES)]
    steps += [conv(rb, cg) for rb in range(CHUNK // conv_rows) for cg in range(W_CONV // LANES)]
    steps += [conv_finish] + [pool(gi, w) for gi, w in enumerate(POOL_WINDOWS)]
    steps += [gating_inputs] + [gating(n) for n in range(CHUNK // SGU_CHUNK)]
    return steps


def _merge_steps(rows, seg, attention, br_ref, merge_refs, work):
    x_ref, mod_ref, g1_ref, g2_ref, wg_ref, wb_ref, wo_ref, wrh_ref, wrl_ref, xm_ref, h2_ref, afft_ref = merge_refs
    h_scr, merged_scr, h2lo_scr = work

    def norm1():
        h_scr[...] = _norm_modulate(x_ref[rows, :], g1_ref[...], _mod_row(mod_ref, seg, 0),
                                    _mod_row(mod_ref, seg, 1)).astype(BF16)

    def branch(n):
        def run():
            z_gate = _dot(h_scr[...], wg_ref[:, n * D_MODEL:(n + 1) * D_MODEL])
            gated = _sigmoid(z_gate) * _dot(attention() if n == 0 else br_ref[n - 1], wb_ref[n])
            merged_scr[...] = gated if n == 0 else merged_scr[...] + gated
        return run

    def project():
        update = _mod_row(mod_ref, seg, 2) * _dot(merged_scr[...].astype(BF16), wo_ref[...])
        xm_ref[rows, :] = x_ref[rows, :] + update

    def norm2():
        h2 = _norm_modulate(xm_ref[rows, :], g2_ref[...], _mod_row(mod_ref, seg, 3), _mod_row(mod_ref, seg, 4))
        h2_hi = h2.astype(BF16)
        h2_ref[rows, :] = h2_hi
        h2lo_scr[...] = (h2 - h2_hi.astype(F32)).astype(BF16)

    def router():
        h2_hi = h2_ref[rows, :]
        h2_lo = h2lo_scr[...]
        logits = _dot(h2_hi, wrh_ref[...]) + (_dot(h2_lo, wrh_ref[...]) + _dot(h2_hi, wrl_ref[...]))
        lane = lax.broadcasted_iota(jnp.int32, logits.shape, 1)
        logits = jnp.where(lane < N_EXPERTS, logits, NEG_INF)
        e = jnp.exp(logits - jnp.max(logits, axis=-1, keepdims=True))
        aff = e / jnp.sum(e, axis=-1, keepdims=True)
        afft_ref[:, rows] = aff.T[:N_EXPERTS, :]

    return [norm1] + [branch(n) for n in range(N_BRANCH)] + [project, norm2, router]


def _mixmerge_body(x_ref, attc_ref, attl_ref, a_ref, g_ref, p_ref, u_ref, v_ref,
                   ap_ref, gp_ref, pp_ref, an_ref, gn_ref, pn_ref,
                   cw_ref, cb_ref, clg_ref, clb_ref, pw_ref, ps_ref, slg_ref, slb_ref, sw_ref, sbt_ref,
                   a2_ref, g2c_ref, p2_ref, u2_ref, v2_ref, an2_ref, gn2_ref, pn2_ref,
                   mod_ref, g1_ref, g2_ref, wg_ref, wb_ref, wo_ref, wrh_ref, wrl_ref,
                   xm_ref, h2_ref, afft_ref,
                   ypad, zpad, yshift, acc_scr, ug_scr, vn_scr, h_scr, merged_scr, h2lo_scr, br_cur, br_next):
    i = pl.program_id(0)
    tm = 2 * CHUNK
    seg = _segment_of_row(i * tm)
    is_ctx = i < N_CTX // tm
    mix = (cw_ref, cb_ref, clg_ref, clb_ref, pw_ref, ps_ref, slg_ref, slb_ref, sw_ref, sbt_ref)
    mix_work = (ypad, zpad, yshift, acc_scr, ug_scr, vn_scr)
    merge_refs = (x_ref, mod_ref, g1_ref, g2_ref, wg_ref, wb_ref, wo_ref, wrh_ref, wrl_ref,
                  xm_ref, h2_ref, afft_ref)
    merge_work = (h_scr, merged_scr, h2lo_scr)
    first, second = slice(0, CHUNK), slice(CHUNK, 2 * CHUNK)
    mid_prev, mid_next = slice(CHUNK - HALO, CHUNK), slice(CHUNK, CHUNK + HALO)
    tail = slice(2 * CHUNK - HALO, 2 * CHUNK)

    def tile_rows(rows):
        return {"a": lambda: a_ref[rows, :], "g": lambda: g_ref[rows, :], "p": lambda: p_ref[rows, :],
                "u": lambda: u_ref[rows, :], "v": lambda: v_ref[rows, :]}

    def halo(refs, rows=slice(None)):
        return lambda: tuple(r[rows, :] for r in refs)

    tile_agp = (a_ref, g_ref, p_ref)
    src0 = dict(tile_rows(first), prev=halo((ap_ref, gp_ref, pp_ref)), nxt=halo(tile_agp, mid_next))
    src1 = dict(tile_rows(second), prev=halo(tile_agp, mid_prev), nxt=halo((an_ref, gn_ref, pn_ref)))
    src2 = {"a": lambda: a2_ref[...], "g": lambda: g2c_ref[...], "p": lambda: p2_ref[...],
            "u": lambda: u2_ref[...], "v": lambda: v2_ref[...],
            "prev": halo(tile_agp, tail), "nxt": halo((an2_ref, gn2_ref, pn2_ref))}

    @pl.when(i == 0)
    def _():
        for step in _mixers_steps(0, src0, mix, mix_work, br_next):
            step()

    def attention(rows):
        return lambda: jnp.where(is_ctx, attc_ref[rows, :], attl_ref[rows, :])

    _interleave(_merge_steps(first, seg, attention(first), br_next, merge_refs, merge_work),
                _mixers_steps(2 * i + 1, src1, mix, mix_work, br_cur))
    _interleave(_merge_steps(second, seg, attention(second), br_cur, merge_refs, merge_work),
                _mixers_steps(2 * i + 2, src2, mix, mix_work, br_next))


def _mixmerge(x, att_ctx, att_lat, proj, mod, norm1_g, norm2_g, conv_w, conv_b, conv_ln_g, conv_ln_b,
              pool_w_bf, pool_scale, sgu_ln_g, sgu_ln_b, sgu_w_bf, sgu_bt, w_in_bf, w_branch_bf, w_out_bf,
              w_router_hi, w_router_lo, l):
    tm = 512
    n_ctx_tiles = N_CTX // tm
    n_lat_tiles = N_LAT // tm
    halos_per_tile = tm // HALO
    n_halo = N_TOK // HALO

    def rows(w):
        return pl.BlockSpec((tm, w), lambda i: (i, 0))

    def main(col):
        return pl.BlockSpec((tm, COL_BLK), lambda i: (i, col))

    def prev(col):
        return pl.BlockSpec((HALO, COL_BLK), lambda i: (jnp.maximum(i * halos_per_tile - 1, 0), col))

    def nxt(col):
        return pl.BlockSpec((HALO, COL_BLK), lambda i: (jnp.minimum((i + 1) * halos_per_tile, n_halo - 1), col))

    def ahead(col):
        return pl.BlockSpec((CHUNK, COL_BLK), lambda i: (jnp.minimum(2 * i + 2, N_TOK // CHUNK - 1), col))

    def ahead_nxt(col):
        return pl.BlockSpec(
            (HALO, COL_BLK),
            lambda i: (jnp.minimum((i + 1) * halos_per_tile + CHUNK // HALO, n_halo - 1), col))

    def vec(n):
        return pl.BlockSpec((None, 1, n), lambda i: (l, 0, 0))

    def const(shape, index_map):
        return pl.BlockSpec(shape, index_map, pipeline_mode=pl.Buffered(1))

    return pl.pallas_call(
        _mixmerge_body,
        grid=(N_TOK // tm,),
        in_specs=[
            rows(D_MODEL),
            pl.BlockSpec((tm, W_ATT), lambda i: (jnp.minimum(i, n_ctx_tiles - 1), 0)),
            pl.BlockSpec((tm, W_ATT), lambda i: (jnp.clip(i - n_ctx_tiles, 0, n_lat_tiles - 1), 0)),
            main(3), main(4), main(5), main(6), main(7),
            prev(3), prev(4), prev(5), nxt(3), nxt(4), nxt(5),
            pl.BlockSpec((None, CONV_K, W_CONV), lambda i: (l, 0, 0)),
            vec(W_CONV), vec(W_CONV), vec(W_CONV),
            pl.BlockSpec((None, len(POOL_WINDOWS), POOL_GROUP, POOL_GROUP), lambda i: (l, 0, 0, 0)),
            vec(W_POOL), vec(W_SGU), vec(W_SGU),
            pl.BlockSpec((None, SGU_GROUPS, SGU_CHUNK, SGU_CHUNK), lambda i: (l, 0, 0, 0)),
            pl.BlockSpec((None, SGU_CHUNK, SGU_GROUPS), lambda i: (l, 0, 0)),
            ahead(3), ahead(4), ahead(5), ahead(6), ahead(7), ahead_nxt(3), ahead_nxt(4), ahead_nxt(5),
            pl.BlockSpec((None, 8, 6 * D_MODEL), lambda i: (l, 0, 0)),
            vec(D_MODEL), vec(D_MODEL),
            const((None, D_MODEL, N_BRANCH * D_MODEL), lambda i: (l, 0, 1)),
            const((None, N_BRANCH, BRANCH_W, D_MODEL), lambda i: (l, 0, 0, 0)),
            const((None, D_MODEL, D_MODEL), lambda i: (l, 0, 0)),
            const((None, D_MODEL, LANES), lambda i: (l, 0, 0)),
            const((None, D_MODEL, LANES), lambda i: (l, 0, 0)),
        ],
        out_specs=[rows(D_MODEL), rows(D_MODEL), pl.BlockSpec((N_EXPERTS, tm), lambda i: (0, i))],
        out_shape=[
            jax.ShapeDtypeStruct((N_TOK, D_MODEL), F32),
            jax.ShapeDtypeStruct((N_TOK, D_MODEL), BF16),
            jax.ShapeDtypeStruct((N_EXPERTS, N_TOK), F32),
        ],
        scratch_shapes=[
            pltpu.VMEM((CHUNK + 2 * HALO, W_CONV), F32),
            pltpu.VMEM((CHUNK + 2 * HALO, W_POOL), F32),
            pltpu.VMEM((SUBLANES, CHUNK + 2 * HALO - SUBLANES, W_CONV), F32),
            pltpu.VMEM((CHUNK, W_CONV), F32),
            pltpu.VMEM((CHUNK, W_SGU), F32),
            pltpu.VMEM((CHUNK, W_SGU), BF16),
            pltpu.VMEM((CHUNK, D_MODEL), BF16),
            pltpu.VMEM((CHUNK, D_MODEL), F32),
            pltpu.VMEM((CHUNK, D_MODEL), BF16),
            pltpu.VMEM((3, CHUNK, BRANCH_W), BF16),
            pltpu.VMEM((3, CHUNK, BRANCH_W), BF16),
        ],
        compiler_params=_cparams(1, 56),
        name="mixmerge",
    )(x, att_ctx, att_lat, *([proj] * 11), conv_w, conv_b, conv_ln_g, conv_ln_b, pool_w_bf, pool_scale,
      sgu_ln_g, sgu_ln_b, sgu_w_bf, sgu_bt, *([proj] * 8), mod, norm1_g, norm2_g, w_in_bf, w_branch_bf,
      w_out_bf, w_router_hi, w_router_lo)


def _route_body(afft_ref, tri_ref, pos_ref, gate_ref, *, n_seq, seq_len, cap):
    a = jnp.concatenate([afft_ref[:, s * seq_len:(s + 1) * seq_len] for s in range(n_seq)], axis=0)
    n_rows = n_seq * N_EXPERTS
    capf = float(cap)

    def count(mask):
        return jnp.sum(jnp.where(mask, 1.0, 0.0), axis=1, keepdims=True)

    def as_float(bits):
        return lax.bitcast_convert_type(bits, F32)

    thr = jnp.zeros((n_rows, 1), jnp.int32)
    for bit in range(30, -1, -1):
        cand = thr | (1 << bit)
        thr = jnp.where(count(a >= as_float(cand)) >= capf, cand, thr)
    above = a >= as_float(thr + 1)
    tie = jnp.logical_and(a >= as_float(thr), jnp.logical_not(above))
    need = capf - count(above)

    idx = lax.broadcasted_iota(jnp.int32, (n_rows, seq_len), 1)
    last = jnp.zeros((n_rows, 1), jnp.int32)
    for bit in range(seq_len.bit_length() - 2, -1, -1):
        cand = last | (1 << bit)
        before = jnp.sum(jnp.where(tie, jnp.where(idx < cand, 1.0, 0.0), 0.0), axis=1, keepdims=True)
        last = jnp.where(before < need, cand, last)
    keep = jnp.where(above, 1.0, jnp.where(tie, jnp.where(idx <= last, 1.0, 0.0), 0.0))

    blk = tri_ref.shape[0]
    offset = jnp.zeros((n_rows, 1), F32)
    pos = []
    for c in range(seq_len // blk):
        kb = keep[:, c * blk:(c + 1) * blk]
        pos.append(_dot(kb.astype(BF16), tri_ref[...]) + offset)
        offset = offset + jnp.sum(kb, axis=1, keepdims=True)
    pos = jnp.concatenate(pos, axis=1)
    pos_ref[...] = jnp.where(keep > 0.0, pos, -1.0)
    gate_ref[...] = a


def _route(aff_t, tri, n_seq, seq_len, cap, col0):
    n_rows = n_seq * N_EXPERTS
    width = n_seq * seq_len
    shape = jax.ShapeDtypeStruct((n_rows, seq_len), F32)
    spec = pl.BlockSpec((n_rows, seq_len), lambda i: (0, 0))
    return pl.pallas_call(
        functools.partial(_route_body, n_seq=n_seq, seq_len=seq_len, cap=cap),
        grid=(1,),
        in_specs=[
            pl.BlockSpec((N_EXPERTS, width), lambda i: (0, col0 // width)),
            pl.BlockSpec(tri.shape, lambda i: (0, 0)),
        ],
        out_specs=[spec, spec],
        out_shape=[shape, shape],
        compiler_params=_cparams(1, 32),
        name="route_ctx" if cap == CAP_CTX else "route_lat",
    )(aff_t, tri)


def _slot_onehot(pos_ref, cap, values_ref=None):
    n_tok = pos_ref.shape[1]
    slot = lax.broadcasted_iota(jnp.int32, (cap, n_tok), 0).astype(F32)
    rows = []
    for e in range(N_EXPERTS):
        hit = pos_ref[e:e + 1, :] == slot
        val = 1.0 if values_ref is None else values_ref[e:e + 1, :]
        rows.append(jnp.where(hit, val, 0.0).astype(BF16))
    return jnp.concatenate(rows, axis=0)


def _gather_ctx_body(pos_ref, h_ref, o_ref):
    onehot = _slot_onehot(pos_ref, CAP_CTX)
    o_ref[...] = _dot(onehot, h_ref[...]).astype(BF16).reshape(N_EXPERTS, CAP_CTX, D_MODEL)


def _gather_ctx(pos, h2):
    return pl.pallas_call(
        _gather_ctx_body,
        grid=(BATCH,),
        in_specs=[
            pl.BlockSpec((N_EXPERTS, SEQ), lambda s: (s, 0)),
            pl.BlockSpec((SEQ, D_MODEL), lambda s: (s, 0)),
        ],
        out_specs=pl.BlockSpec((N_EXPERTS, CAP_CTX, D_MODEL), lambda s: (0, s, 0)),
        out_shape=jax.ShapeDtypeStruct((N_EXPERTS, BATCH * CAP_CTX, D_MODEL), BF16),
        compiler_params=_cparams(1, 32),
        name="gather_ctx",
    )(pos, h2)


def _gather_lat_body(pos_ref, h_ref, o_ref):
    e = pl.program_id(1)
    slot = lax.broadcasted_iota(jnp.int32, (CAP_LAT, DEC_SEQ), 0).astype(F32)
    onehot = jnp.where(pos_ref[pl.ds(e, 1), :] == slot, 1.0, 0.0).astype(BF16)
    o_ref[...] = _dot(onehot, h_ref[...]).astype(BF16)


def _gather_lat(pos, h2):
    blk0 = N_CTX // DEC_SEQ
    return pl.pallas_call(
        _gather_lat_body,
        grid=(DEC_BATCH, N_EXPERTS),
        in_specs=[
            pl.BlockSpec((N_EXPERTS, DEC_SEQ), lambda b, e: (b, 0)),
            pl.BlockSpec((DEC_SEQ, D_MODEL), lambda b, e: (blk0 + b, 0)),
        ],
        out_specs=pl.BlockSpec((None, CAP_LAT, D_MODEL), lambda b, e: (e, b, 0)),
        out_shape=jax.ShapeDtypeStruct((N_EXPERTS, DEC_BATCH * CAP_LAT, D_MODEL), BF16),
        compiler_params=_cparams(2, 40),
        name="gather_lat",
    )(pos, h2)


def _ffn_body(xc_ref, xl_ref, w1_ref, w3_ref, w2_ref, yc_ref, yl_ref):
    w1 = w1_ref[...].astype(BF16)
    w3 = w3_ref[...].astype(BF16)
    w2 = w2_ref[...].astype(BF16)
    rows = 512
    for x_ref, y_ref in ((xc_ref, yc_ref), (xl_ref, yl_ref)):
        for r0 in range(0, x_ref.shape[0], rows):
            x = x_ref[r0:r0 + rows, :]
            a = _dot(x, w1)
            act = (a * _sigmoid(a) * _dot(x, w3)).astype(BF16)
            y_ref[r0:r0 + rows, :] = _dot(act, w2).astype(BF16)


def _ffn(xg_ctx, xg_lat, w1, w3, w2, l):
    n_c, n_l = xg_ctx.shape[1], xg_lat.shape[1]

    def wspec():
        return pl.BlockSpec((None, None, D_MODEL, D_MODEL), lambda e: (l, e, 0, 0))

    def xspec(n):
        return pl.BlockSpec((None, n, D_MODEL), lambda e: (e, 0, 0))

    return pl.pallas_call(
        _ffn_body,
        grid=(N_EXPERTS,),
        in_specs=[xspec(n_c), xspec(n_l), wspec(), wspec(), wspec()],
        out_specs=[xspec(n_c), xspec(n_l)],
        out_shape=[jax.ShapeDtypeStruct(xg_ctx.shape, BF16), jax.ShapeDtypeStruct(xg_lat.shape, BF16)],
        compiler_params=_cparams(1, 56),
        name="ffn",
    )(xg_ctx, xg_lat, w1, w3, w2)


def _combine_body(pos_ref, gate_ref, y_ref, x_ref, mod_ref, *rest, cap, seg0):
    o_ref = rest[-1]
    weights = _slot_onehot(pos_ref, cap, gate_ref)
    y = y_ref[...].reshape(N_EXPERTS * cap, D_MODEL)
    seg = seg0 + pl.program_id(0) if seg0 else 0
    out = x_ref[...] + _mod_row(mod_ref, seg, 5) * _dot_tn(weights, y)
    if len(rest) == 2:
        out = out * lax.rsqrt(jnp.mean(out * out, axis=-1, keepdims=True) + EPS) * rest[0][...]
    o_ref[...] = out


def _combine_ctx(pos, gate, y, x_mid, mod, l, final_g=None):
    last = final_g is not None
    return pl.pallas_call(
        functools.partial(_combine_body, cap=CAP_CTX, seg0=0),
        grid=(BATCH,),
        in_specs=[
            pl.BlockSpec((N_EXPERTS, SEQ), lambda s: (s, 0)),
            pl.BlockSpec((N_EXPERTS, SEQ), lambda s: (s, 0)),
            pl.BlockSpec((N_EXPERTS, CAP_CTX, D_MODEL), lambda s: (0, s, 0)),
            pl.BlockSpec((SEQ, D_MODEL), lambda s: (s, 0)),
            pl.BlockSpec((None, 8, 6 * D_MODEL), lambda s: (l, 0, 0)),
        ] + ([pl.BlockSpec((1, D_MODEL), lambda s: (0, 0))] if last else []),
        out_specs=pl.BlockSpec((SEQ, D_MODEL), lambda s: (s, 0)),
        out_shape=jax.ShapeDtypeStruct((N_CTX if last else N_TOK, D_MODEL), F32),
        input_output_aliases={} if last else {3: 0},
        compiler_params=_cparams(1, 32),
        name="combine_ctx",
    )(pos, gate, y, x_mid, mod, *([final_g] if last else []))


def _combine_lat(pos, gate, y, x_mid, mod, l, final_g=None):
    last = final_g is not None
    tiles = DEC_SEQ // CHUNK
    blk0 = N_CTX // CHUNK
    return pl.pallas_call(
        functools.partial(_combine_body, cap=CAP_LAT, seg0=1),
        grid=(DEC_BATCH, tiles),
        in_specs=[
            pl.BlockSpec((N_EXPERTS, CHUNK), lambda b, t: (b, t)),
            pl.BlockSpec((N_EXPERTS, CHUNK), lambda b, t: (b, t)),
            pl.BlockSpec((N_EXPERTS, CAP_LAT, D_MODEL), lambda b, t: (0, b, 0)),
            pl.BlockSpec((CHUNK, D_MODEL), lambda b, t: (blk0 + b * tiles + t, 0)),
            pl.BlockSpec((None, 8, 6 * D_MODEL), lambda b, t: (l, 0, 0)),
        ] + ([pl.BlockSpec((1, D_MODEL), lambda b, t: (0, 0))] if last else []),
        out_specs=pl.BlockSpec((CHUNK, D_MODEL), lambda b, t: ((0 if last else blk0) + b * tiles + t, 0)),
        out_shape=jax.ShapeDtypeStruct((N_LAT if last else N_TOK, D_MODEL), F32),
        input_output_aliases={} if last else {3: 0},
        compiler_params=_cparams(2, 48),
        name="combine_lat",
    )(pos, gate, y, x_mid, mod, *([final_g] if last else []))


def kernel(x_prompt, x_sample, cache_k, cache_v, c, c_ctx, w_ada, b_ada, norm1_g, w_in, rpb, conv_w, conv_b,
           conv_ln_g, conv_ln_b, pool_w, pool_scale, sgu_ln_g, sgu_ln_b, sgu_w, sgu_b, w_branch, w_out,
           norm2_g, w_router, w1, w3, w2, final_g):
    def vecs(a):
        return a.reshape(DEPTH, 1, a.shape[-1])

    cond8 = jnp.concatenate([c_ctx[None, :], c, jnp.zeros((8 - 1 - DEC_BATCH, D_MODEL), F32)], axis=0)
    mod = _adaln(cond8, w_ada, b_ada)
    table = _rpb_table(rpb)
    cos_np, sin_np = _rope_tables()
    cos, sin = jnp.asarray(cos_np), jnp.asarray(sin_np)
    tri = jnp.asarray(np.triu(np.ones((CHUNK, CHUNK), np.float32), 1), BF16)

    w_in_bf = w_in.astype(BF16)
    w_branch_bf = w_branch.astype(BF16)
    w_out_bf = w_out.astype(BF16)
    pool_w_bf = pool_w.astype(BF16)
    sgu_w_bf = sgu_w.astype(BF16)
    sgu_bt = sgu_b.transpose(0, 2, 1)
    w_router_pad = jnp.pad(w_router, ((0, 0), (0, 0), (0, LANES - N_EXPERTS)))
    w_router_hi = w_router_pad.astype(BF16)
    w_router_lo = (w_router_pad - w_router_hi.astype(F32)).astype(BF16)
    norm1_v, norm2_v = vecs(norm1_g), vecs(norm2_g)
    conv_b_v, conv_ln_g_v, conv_ln_b_v = vecs(conv_b), vecs(conv_ln_g), vecs(conv_ln_b)
    pool_scale_v, sgu_ln_g_v, sgu_ln_b_v = vecs(pool_scale), vecs(sgu_ln_g), vecs(sgu_ln_b)

    x = jnp.concatenate([x_prompt.reshape(N_CTX, D_MODEL), x_sample.reshape(N_LAT, D_MODEL)], axis=0)
    projs = []
    for l in range(DEPTH):
        proj = _proj(x, norm1_v, mod, w_in_bf, l)
        projs.append(proj)
        att_ctx = _attn_ctx(proj)
        att_lat = _attn_lat(proj, cache_k, cache_v, table, cos, sin, l)
        x_mid, h2, aff_t = _mixmerge(x, att_ctx, att_lat, proj, mod, norm1_v, norm2_v, conv_w, conv_b_v,
                                     conv_ln_g_v, conv_ln_b_v, pool_w_bf, pool_scale_v, sgu_ln_g_v, sgu_ln_b_v,
                                     sgu_w_bf, sgu_bt, w_in_bf, w_branch_bf, w_out_bf, w_router_hi,
                                     w_router_lo, l)
        pos_c, gate_c = _route(aff_t, tri, BATCH, SEQ, CAP_CTX, 0)
        pos_l, gate_l = _route(aff_t, tri, DEC_BATCH, DEC_SEQ, CAP_LAT, N_CTX)
        y_c, y_l = _ffn(_gather_ctx(pos_c, h2), _gather_lat(pos_l, h2), w1, w3, w2, l)
        if l < DEPTH - 1:
            x = _combine_ctx(pos_c, gate_c, y_c, x_mid, mod, l)
            x = _combine_lat(pos_l, gate_l, y_l, x, mod, l)
        else:
            y_prompt = _combine_ctx(pos_c, gate_c, y_c, x_mid, mod, l, final_g[None, :])
            y_sample = _combine_lat(pos_l, gate_l, y_l, x_mid, mod, l, final_g[None, :])

    y_prompt = y_prompt.reshape(BATCH, SEQ, D_MODEL)
    y_sample = y_sample.reshape(DEC_BATCH, DEC_SEQ, D_MODEL)
    new_k, new_v = _kv_out(projs)
    return (y_prompt, y_sample, new_k, new_v)
```

```python
import functools

import numpy as np
import jax
import jax.numpy as jnp
from jax import lax
from jax.experimental import pallas as pl
from jax.experimental.pallas import tpu as pltpu

F32 = jnp.float32
BF16 = jnp.bfloat16

D_MODEL = 1024
BATCH = 32
SEQ = 256
DEPTH = 4
DEC_BATCH = 2
DEC_SEQ = 2048
PAST_LEN = 512
GRID_W = 64
GRID_ROWS = DEC_SEQ // GRID_W
N_HEADS = 8
HEAD_DIM = 64
W_ATT = N_HEADS * HEAD_DIM
NA_ROWS = 8
NA_COLS = 16
ROPE_BASE = 10000.0
W_CONV = 512
CONV_K = 31
W_POOL = 512
POOL_WINDOWS = (2, 4, 8, 16)
POOL_GROUP = 128
W_SGU = 512
SGU_GROUPS = 4
SGU_CHUNK = 128
N_BRANCH = 4
BRANCH_W = 512
N_EXPERTS = 16
EC_CAPACITY = 2
EPS = 1e-6
NEG_INF = -1e30

N_CTX = BATCH * SEQ
N_LAT = DEC_BATCH * DEC_SEQ
N_TOK = N_CTX + N_LAT
D_PROJ = 4096
COL_BLK = 512
CAP_CTX = EC_CAPACITY * SEQ // N_EXPERTS
CAP_LAT = EC_CAPACITY * DEC_SEQ // N_EXPERTS
N_ROW_OFF = 2 * NA_ROWS
Q_ROWS = 4
KEY_ROWS = 12
MOE_SEQS = 4
KV_SEQS = 4
HEAD_GROUP = 4
HALO = 16
CHUNK = 256
LANES = 128
SUBLANES = 8
VMEM_LIMIT_CAP = 56 * 1024 * 1024

HIGHEST = lax.Precision.HIGHEST


def _cparams(n_grid, vmem_mb):
    return pltpu.CompilerParams(
        dimension_semantics=("arbitrary",) * n_grid,
        vmem_limit_bytes=min(vmem_mb * 1024 * 1024, VMEM_LIMIT_CAP))


def _dot(a, b):
    return jnp.dot(a, b, preferred_element_type=F32)


def _dot_tn(a, b):
    return lax.dot_general(a, b, (((0,), (0,)), ((), ())), preferred_element_type=F32)


def _sigmoid(x):
    return 0.5 * jnp.tanh(0.5 * x) + 0.5


def _segment_of_row(row0):
    return jnp.where(row0 < N_CTX, 0, 1 + (row0 - N_CTX) // DEC_SEQ)


def _mod_row(mod_ref, seg, k):
    return mod_ref[pl.ds(seg, 1), k * D_MODEL:(k + 1) * D_MODEL]


def _norm_modulate(x, g, shift, scale):
    y = x * lax.rsqrt(jnp.mean(x * x, axis=-1, keepdims=True) + EPS) * g
    return y * (1.0 + scale) + shift


def _layernorm(x, g, b):
    mu = jnp.mean(x, axis=-1, keepdims=True)
    xc = x - mu
    var = jnp.mean(xc * xc, axis=-1, keepdims=True)
    return xc * lax.rsqrt(var + EPS) * g + b


def _adaln_body(cond_ref, w_ref, b_ref, o_ref):
    c = cond_ref[...]
    a = (c * _sigmoid(c)).astype(BF16)
    o_ref[...] = _dot(a, w_ref[...].astype(BF16)) + b_ref[...]


def _adaln(cond8, w_ada, b_ada):
    tn = 1536
    n6 = 6 * D_MODEL
    return pl.pallas_call(
        _adaln_body,
        grid=(DEPTH, n6 // tn),
        in_specs=[
            pl.BlockSpec((8, D_MODEL), lambda l, j: (0, 0)),
            pl.BlockSpec((None, D_MODEL, tn), lambda l, j: (l, 0, j)),
            pl.BlockSpec((None, 1, tn), lambda l, j: (l, 0, j)),
        ],
        out_specs=pl.BlockSpec((None, 8, tn), lambda l, j: (l, 0, j)),
        out_shape=jax.ShapeDtypeStruct((DEPTH, 8, n6), F32),
        compiler_params=_cparams(2, 32),
        name="adaln",
    )(cond8, w_ada, b_ada.reshape(DEPTH, 1, n6))


def _rpb_onehots():
    cq = np.arange(GRID_W)[None, :]
    kc = np.arange(GRID_W)[:, None]
    start = np.clip(cq - NA_COLS // 2, 0, GRID_W - NA_COLS)
    valid = (kc >= start) & (kc < start + NA_COLS)
    d_col = np.clip(kc - cq, -(NA_COLS - 1), NA_COLS - 1) + (NA_COLS - 1)
    left = np.zeros((32, GRID_W, 2 * GRID_W), np.float32)
    right = np.zeros((32, GRID_W, 2 * GRID_W), np.float32)
    mask = np.zeros((GRID_W, 2 * GRID_W), np.float32)
    for j in range(2 * NA_COLS - 1):
        hit = (d_col == j) & valid
        left[j, :, :GRID_W] = hit
        right[j, :, GRID_W:] = hit
    mask[:, :GRID_W] = np.where(valid, 0.0, NEG_INF)
    mask[:, GRID_W:] = np.where(valid, 0.0, NEG_INF)
    n = GRID_W * 2 * GRID_W
    return left.reshape(32, n), right.reshape(32, n), mask.reshape(1, n)


def _rpb_body(a_ref, b_ref, left_ref, right_ref, mask_ref, o_ref):
    t = jnp.dot(a_ref[...], left_ref[...], precision=HIGHEST, preferred_element_type=F32)
    t = t + jnp.dot(b_ref[...], right_ref[...], precision=HIGHEST, preferred_element_type=F32)
    o_ref[...] = t + mask_ref[...]


def _rpb_table(rpb):
    left, right, mask = _rpb_onehots()
    n_rows = N_HEADS * N_ROW_OFF
    rpb_p = jnp.pad(rpb, ((0, 0), (0, 0), (0, 0), (0, 1)))
    a = jnp.pad(rpb_p, ((0, 0), (0, 0), (0, 1), (0, 0))).reshape(DEPTH, n_rows, 32)
    b = jnp.pad(rpb_p, ((0, 0), (0, 0), (1, 0), (0, 0))).reshape(DEPTH, n_rows, 32)
    n = left.shape[1]
    out = pl.pallas_call(
        _rpb_body,
        grid=(DEPTH,),
        in_specs=[
            pl.BlockSpec((None, n_rows, 32), lambda l: (l, 0, 0)),
            pl.BlockSpec((None, n_rows, 32), lambda l: (l, 0, 0)),
            pl.BlockSpec((32, n), lambda l: (0, 0)),
            pl.BlockSpec((32, n), lambda l: (0, 0)),
            pl.BlockSpec((1, n), lambda l: (0, 0)),
        ],
        out_specs=pl.BlockSpec((None, n_rows, n), lambda l: (l, 0, 0)),
        out_shape=jax.ShapeDtypeStruct((DEPTH, n_rows, n), F32),
        compiler_params=_cparams(1, 32),
        name="rpb_table",
    )(a, b, jnp.asarray(left), jnp.asarray(right), jnp.asarray(mask))
    return out.reshape(DEPTH, N_HEADS, N_ROW_OFF, GRID_W, 2 * GRID_W)


def _proj_body(x_ref, g_ref, mod_ref, w_ref, o_ref, h_scr, *, tm):
    i = pl.program_id(0)

    @pl.when(pl.program_id(1) == 0)
    def _():
        seg = _segment_of_row(i * tm)
        h = _norm_modulate(x_ref[...], g_ref[...], _mod_row(mod_ref, seg, 0), _mod_row(mod_ref, seg, 1))
        h_scr[...] = h.astype(BF16)

    o_ref[...] = _dot(h_scr[...], w_ref[...].astype(BF16)).astype(BF16)


def _proj(x, norm_g, mod, w_in, l):
    tm, tn = 1024, 1024
    return pl.pallas_call(
        functools.partial(_proj_body, tm=tm),
        grid=(N_TOK // tm, D_PROJ // tn),
        in_specs=[
            pl.BlockSpec((tm, D_MODEL), lambda i, j: (i, 0)),
            pl.BlockSpec((None, 1, D_MODEL), lambda i, j: (l, 0, 0)),
            pl.BlockSpec((None, 8, 6 * D_MODEL), lambda i, j: (l, 0, 0)),
            pl.BlockSpec((None, D_MODEL, tn), lambda i, j: (l, 0, j)),
        ],
        out_specs=pl.BlockSpec((tm, tn), lambda i, j: (i, j)),
        out_shape=jax.ShapeDtypeStruct((N_TOK, D_PROJ), BF16),
        scratch_shapes=[pltpu.VMEM((tm, D_MODEL), BF16)],
        compiler_params=_cparams(2, 40),
        name="proj",
    )(x, norm_g, mod, w_in)


def _head_rows_in_pair(x_t, h):
    rows = x_t[h * HEAD_DIM:(h + 1) * HEAD_DIM, :]
    zeros = jnp.zeros_like(rows)
    return jnp.concatenate([rows, zeros] if h % 2 == 0 else [zeros, rows], axis=0)


def _pair_lanes(h):
    return slice((h // 2) * LANES, (h // 2 + 1) * LANES)


def _attn_ctx_body(q_ref, k_ref, v_ref, o_ref):
    k = k_ref[...]
    v = v_ref[...]
    q_t = (q_ref[...].astype(F32) * (HEAD_DIM ** -0.5)).T.astype(BF16)
    v_t = v.astype(F32).T.astype(BF16)
    heads = range(N_HEADS)
    s_t = [_dot(k[:, _pair_lanes(h)], _head_rows_in_pair(q_t, h)) for h in heads]
    e_t = [jnp.exp(s - jnp.max(s, axis=0, keepdims=True)) for s in s_t]
    denom = [jnp.sum(e, axis=0, keepdims=True) for e in e_t]
    o_t = [_dot(v_t[h * HEAD_DIM:(h + 1) * HEAD_DIM, :], e_t[h].astype(BF16)) / denom[h] for h in heads]
    o_ref[...] = jnp.concatenate(o_t, axis=0).T.astype(BF16)


def _attn_ctx(proj):
    return pl.pallas_call(
        _attn_ctx_body,
        grid=(BATCH,),
        in_specs=[
            pl.BlockSpec((SEQ, COL_BLK), lambda b: (b, 0)),
            pl.BlockSpec((SEQ, COL_BLK), lambda b: (b, 1)),
            pl.BlockSpec((SEQ, COL_BLK), lambda b: (b, 2)),
        ],
        out_specs=pl.BlockSpec((SEQ, W_ATT), lambda b: (b, 0)),
        out_shape=jax.ShapeDtypeStruct((N_CTX, W_ATT), BF16),
        compiler_params=_cparams(1, 32),
        name="attn_ctx",
    )(proj, proj, proj)


def _kv_out_body(*refs):
    kv_refs, (ck_ref, cv_ref) = refs[:2 * DEPTH], refs[2 * DEPTH:]
    l = pl.program_id(0)
    for j in range(DEPTH):
        @pl.when(l == j)
        def _(j=j):
            for s in range(KV_SEQS):
                rows = slice(s * SEQ, (s + 1) * SEQ)
                k = kv_refs[2 * j][rows, :]
                v = kv_refs[2 * j + 1][rows, :]
                for h in range(N_HEADS):
                    sl = slice(h * HEAD_DIM, (h + 1) * HEAD_DIM)
                    ck_ref[s, h] = k[:, sl].astype(F32)
                    cv_ref[s, h] = v[:, sl].astype(F32)


def _kv_out(projs):
    steps = BATCH // KV_SEQS

    def src(j, col):
        return pl.BlockSpec((KV_SEQS * SEQ, COL_BLK),
                            lambda l, b: (jnp.where(l == j, b, jnp.where(l < j, 0, steps - 1)), col))

    cache_spec = pl.BlockSpec((KV_SEQS, None, N_HEADS, SEQ, HEAD_DIM), lambda l, b: (b, l, 0, 0, 0))
    cache_shape = jax.ShapeDtypeStruct((BATCH, DEPTH, N_HEADS, SEQ, HEAD_DIM), F32)
    in_specs, args = [], []
    for j in range(DEPTH):
        in_specs += [src(j, 1), src(j, 2)]
        args += [projs[j], projs[j]]
    return pl.pallas_call(
        _kv_out_body,
        grid=(DEPTH, steps),
        in_specs=in_specs,
        out_specs=[cache_spec, cache_spec],
        out_shape=[cache_shape, cache_shape],
        compiler_params=_cparams(2, 48),
        name="kv_out",
    )(*args)


def _rope_tables():
    t = np.arange(DEC_SEQ)
    half = HEAD_DIM // 2
    nf = half // 2
    inv = (1.0 / (ROPE_BASE ** (np.arange(nf) / nf))).astype(np.float32)
    cos = np.zeros((DEC_SEQ, HEAD_DIM), np.float32)
    sin = np.zeros((DEC_SEQ, HEAD_DIM), np.float32)
    for blk, pos in enumerate((t // GRID_W, t % GRID_W)):
        ang = (pos[:, None].astype(np.float32) * inv[None, :]).astype(np.float32)
        c, s = np.cos(ang), np.sin(ang)
        cos[:, blk * half:(blk + 1) * half] = np.concatenate([c, c], axis=1)
        sin[:, blk * half:(blk + 1) * half] = np.concatenate([-s, s], axis=1)
    return np.tile(cos, (1, 2)), np.tile(sin, (1, 2))


def _rope(x, cos, sin):
    cos = jnp.concatenate([cos] * (W_ATT // LANES), axis=1)
    sin = jnp.concatenate([sin] * (W_ATT // LANES), axis=1)
    lane = lax.broadcasted_iota(jnp.int32, x.shape, 1)
    nf = HEAD_DIM // 4
    partner = jnp.where(lane % (2 * nf) < nf, pltpu.roll(x, W_ATT - nf, 1), pltpu.roll(x, nf, 1))
    return x * cos + partner * sin


def _attn_lat_body(q_ref, k_ref, v_ref, ck_ref, cv_ref, tab_ref, cos_ref, sin_ref, o_ref,
                   krot_scr, vt_scr, ck_scr, cvt_scr):
    g = pl.program_id(1)
    blk = Q_ROWS * GRID_W
    n_blk = DEC_SEQ // blk

    @pl.when(g == 0)
    def _():
        def chunk(c, carry):
            rows = pl.ds(pl.multiple_of(c * blk, blk), blk)
            kr = _rope(k_ref[rows, :].astype(F32), cos_ref[rows, :], sin_ref[rows, :])
            krot_scr[rows, :] = kr.astype(BF16)
            vt_scr[c] = v_ref[rows, :].astype(F32).T.astype(BF16)
            return carry
        lax.fori_loop(0, n_blk, chunk, 0)
        for h in range(0, N_HEADS, 2):
            ck_scr[h] = ck_ref[h].astype(BF16)
            ck_scr[h + 1] = ck_ref[h + 1].astype(BF16)
            pair_t = jnp.concatenate([cv_ref[h], cv_ref[h + 1]], axis=1).T
            cvt_scr[h] = pair_t[:HEAD_DIM, :].astype(BF16)
            cvt_scr[h + 1] = pair_t[HEAD_DIM:, :].astype(BF16)

    kh = min(NA_ROWS, GRID_ROWS)
    scale = HEAD_DIM ** -0.5
    r0 = g * Q_ROWS
    blk0 = jnp.clip(g - 1, 0, n_blk - KEY_ROWS // Q_ROWS)
    u = blk0 * Q_ROWS

    qrows = pl.ds(pl.multiple_of(g * blk, blk), blk)
    q = q_ref[...].astype(F32)
    q_rot_t = (_rope(q, cos_ref[qrows, :], sin_ref[qrows, :]) * scale).T.astype(BF16)
    q_plain_t = (q * scale).T.astype(BF16)
    k_win = krot_scr[pl.ds(pl.multiple_of(blk0 * blk, blk), KEY_ROWS * GRID_W), :]

    lane = lax.broadcasted_iota(jnp.int32, (GRID_W, 2 * GRID_W), 1)
    row_masks = []
    for kr in range(KEY_ROWS):
        pair_masks = []
        for ip in range(Q_ROWS // 2):
            m = []
            for i in (2 * ip, 2 * ip + 1):
                start = jnp.clip(r0 + i - kh // 2, 0, GRID_ROWS - kh)
                inside = jnp.logical_and(u + kr >= start, u + kr < start + kh)
                m.append(jnp.where(inside, 0.0, NEG_INF))
            pair_masks.append(jnp.where(lane < GRID_W, m[0], m[1]))
        row_masks.append(jnp.concatenate(pair_masks, axis=1))

    def head_rows(h):
        return slice(h * HEAD_DIM, (h + 1) * HEAD_DIM)

    def local_bias(h):
        bias_rows = []
        for kr in range(KEY_ROWS):
            tiles = []
            for ip in range(Q_ROWS // 2):
                d_row = u + kr - (r0 + 2 * ip) + (NA_ROWS - 1)
                tiles.append(tab_ref[h, jnp.clip(d_row, 0, N_ROW_OFF - 1)])
            bias_rows.append(jnp.concatenate(tiles, axis=1) + row_masks[kr])
        return jnp.concatenate(bias_rows, axis=0)

    outs = []
    for h0 in range(0, N_HEADS, HEAD_GROUP):
        heads = range(h0, h0 + HEAD_GROUP)
        s_loc = [_dot(k_win[:, _pair_lanes(h)], _head_rows_in_pair(q_rot_t, h)) + local_bias(h)
                 for h in heads]
        s_ctx = [_dot(ck_scr[h], q_plain_t[head_rows(h), :]) for h in heads]
        m = [jnp.maximum(jnp.max(a, axis=0, keepdims=True), jnp.max(b, axis=0, keepdims=True))
             for a, b in zip(s_loc, s_ctx)]
        e_loc = [jnp.exp(a - mm) for a, mm in zip(s_loc, m)]
        e_ctx = [jnp.exp(b - mm) for b, mm in zip(s_ctx, m)]
        denom = [jnp.sum(a, axis=0, keepdims=True) + jnp.sum(b, axis=0, keepdims=True)
                 for a, b in zip(e_loc, e_ctx)]
        for i, h in enumerate(heads):
            p_loc = e_loc[i].astype(BF16)
            o_t = _dot(cvt_scr[h], e_ctx[i].astype(BF16))
            for c in range(KEY_ROWS // Q_ROWS):
                o_t = o_t + _dot(vt_scr[blk0 + c, head_rows(h), :], p_loc[c * blk:(c + 1) * blk, :])
            outs.append(o_t / denom[i])
    o_ref[...] = jnp.concatenate(outs, axis=0).T.astype(BF16)


def _attn_lat(proj, cache_k, cache_v, table, cos, sin, l):
    blk = Q_ROWS * GRID_W
    steps = GRID_ROWS // Q_ROWS
    q_blk0 = N_CTX // blk
    kv_blk0 = N_CTX // DEC_SEQ
    cache_spec = pl.BlockSpec((None, None, N_HEADS, PAST_LEN, HEAD_DIM), lambda b, g: (b, l, 0, 0, 0))

    def const(shape, index_map):
        return pl.BlockSpec(shape, index_map, pipeline_mode=pl.Buffered(1))

    return pl.pallas_call(
        _attn_lat_body,
        grid=(DEC_BATCH, steps),
        in_specs=[
            pl.BlockSpec((blk, COL_BLK), lambda b, g: (q_blk0 + b * steps + g, 0)),
            pl.BlockSpec((DEC_SEQ, COL_BLK), lambda b, g: (kv_blk0 + b, 1)),
            pl.BlockSpec((DEC_SEQ, COL_BLK), lambda b, g: (kv_blk0 + b, 2)),
            cache_spec, cache_spec,
            const((None, N_HEADS, N_ROW_OFF, GRID_W, 2 * GRID_W), lambda b, g: (l, 0, 0, 0, 0)),
            const((DEC_SEQ, LANES), lambda b, g: (0, 0)),
            const((DEC_SEQ, LANES), lambda b, g: (0, 0)),
        ],
        out_specs=pl.BlockSpec((blk, W_ATT), lambda b, g: (b * steps + g, 0)),
        out_shape=jax.ShapeDtypeStruct((N_LAT, W_ATT), BF16),
        scratch_shapes=[
            pltpu.VMEM((DEC_SEQ, W_ATT), BF16),
            pltpu.VMEM((DEC_SEQ // blk, W_ATT, blk), BF16),
            pltpu.VMEM((N_HEADS, PAST_LEN, HEAD_DIM), BF16),
            pltpu.VMEM((N_HEADS, HEAD_DIM, PAST_LEN), BF16),
        ],
        compiler_params=_cparams(2, 48),
        name="attn_lat",
    )(proj, proj, proj, cache_k, cache_v, table, cos, sin)


def _interleave(major, minor):
    per = -(-len(minor) // len(major))
    for n, step in enumerate(major):
        step()
        for other in minor[n * per:(n + 1) * per]:
            other()


def _mixers_steps(chunk, src, mix, work, dst):
    cw_ref, cb_ref, clg_ref, clb_ref, pw_ref, ps_ref, slg_ref, slb_ref, sw_ref, sbt_ref = mix
    ypad, zpad, yshift, acc_scr, ug_scr, vn_scr = work
    n_ctx_chunks = N_CTX // CHUNK
    per_seq = DEC_SEQ // CHUNK
    j = (chunk - n_ctx_chunks) % per_seq
    is_lat = chunk >= n_ctx_chunks
    left_ok = jnp.logical_and(is_lat, j > 0)
    right_ok = jnp.logical_and(is_lat, j < per_seq - 1)
    seq_len = jnp.where(is_lat, DEC_SEQ, SEQ)
    chunk_off = jnp.where(is_lat, j * CHUNK, 0)

    def glu(a_, g_):
        return a_.astype(F32) * _sigmoid(g_.astype(F32))

    def fill():
        prev, nxt = src["prev"](), src["nxt"]()
        ypad[0:HALO, :] = jnp.where(left_ok, glu(prev[0], prev[1]), 0.0)
        ypad[HALO:HALO + CHUNK, :] = glu(src["a"](), src["g"]())
        ypad[HALO + CHUNK:, :] = jnp.where(right_ok, glu(nxt[0], nxt[1]), 0.0)
        zpad[0:HALO, :] = jnp.where(left_ok, prev[2].astype(F32), 0.0)
        zpad[HALO:HALO + CHUNK, :] = src["p"]().astype(F32)
        zpad[HALO + CHUNK:, :] = jnp.where(right_ok, nxt[2].astype(F32), 0.0)

    def shift(s):
        def run():
            yshift[s] = ypad[s:s + yshift.shape[1], :]
        return run

    conv_rows = 64
    half_k = CONV_K // 2

    def conv(rb, cg):
        def run():
            lanes = slice(cg * LANES, (cg + 1) * LANES)
            acc = jnp.zeros((conv_rows, LANES), F32)
            for k in range(CONV_K):
                start = HALO + rb * conv_rows + k - half_k
                s = start % SUBLANES
                acc = acc + cw_ref[k:k + 1, lanes] * yshift[s, start - s:start - s + conv_rows, lanes]
            acc_scr[rb * conv_rows:(rb + 1) * conv_rows, lanes] = acc
        return run

    def conv_finish():
        y = _layernorm(acc_scr[...] + cb_ref[...], clg_ref[...], clb_ref[...])
        dst[0] = (y * _sigmoid(y)).astype(BF16)

    def pool(gi, w):
        def run():
            t_seq = lax.broadcasted_iota(jnp.int32, (CHUNK, 1), 0) + chunk_off
            lanes = slice(gi * POOL_GROUP, (gi + 1) * POOL_GROUP)
            lo, hi = -(w // 2), w - w // 2 - 1
            tot = jnp.zeros((CHUNK, POOL_GROUP), F32)
            for jj in range(lo, hi + 1):
                tot = tot + zpad[HALO + jj:HALO + jj + CHUNK, lanes]
            cnt = w - jnp.maximum(0, -lo - t_seq) - jnp.maximum(0, t_seq + hi - (seq_len - 1))
            d = tot / cnt.astype(F32) - zpad[HALO:HALO + CHUNK, lanes]
            dst[1, :, lanes] = (_dot(d.astype(BF16), pw_ref[gi]) * ps_ref[:, lanes]).astype(BF16)
        return run

    def gating_inputs():
        ug_scr[...] = jax.nn.gelu(src["u"]().astype(F32))
        vn_scr[...] = _layernorm(jax.nn.gelu(src["v"]().astype(F32)), slg_ref[...], slb_ref[...]).astype(BF16)

    def gating(n):
        def run():
            pos = slice(n * SGU_CHUNK, (n + 1) * SGU_CHUNK)
            for grp in range(SGU_GROUPS):
                lanes = slice(grp * LANES, (grp + 1) * LANES)
                mixed = _dot(sw_ref[grp], vn_scr[pos, lanes]) + sbt_ref[:, grp:grp + 1]
                dst[2, pos, lanes] = (ug_scr[pos, lanes] * mixed).astype(BF16)
        return run

    steps = [fill] + [shift(s) for s in range(SUBLANES)]
    steps += [conv(rb, cg) for rb in range(CHUNK // conv_rows) for cg in range(W_CONV // LANES)]
    steps += [conv_finish] + [pool(gi, w) for gi, w in enumerate(POOL_WINDOWS)]
    steps += [gating_inputs] + [gating(n) for n in range(CHUNK // SGU_CHUNK)]
    return steps


def _merge_steps(rows, seg, attention, br_ref, merge_refs, work):
    x_ref, mod_ref, g1_ref, g2_ref, wg_ref, wb_ref, wo_ref, wrh_ref, wrl_ref, xm_ref, h2_ref, afft_ref = merge_refs
    h_scr, merged_scr, h2lo_scr = work

    def norm1():
        h_scr[...] = _norm_modulate(x_ref[rows, :], g1_ref[...], _mod_row(mod_ref, seg, 0),
                                    _mod_row(mod_ref, seg, 1)).astype(BF16)

    def branch(n):
        def run():
            half_z = _dot(h_scr[...], wg_ref[:, n * D_MODEL:(n + 1) * D_MODEL])
            gated = (jnp.tanh(half_z) + 1.0) * _dot(attention() if n == 0 else br_ref[n - 1], wb_ref[n])
            merged_scr[...] = gated if n == 0 else merged_scr[...] + gated
        return run

    def project():
        update = _mod_row(mod_ref, seg, 2) * _dot(merged_scr[...].astype(BF16), wo_ref[...])
        xm_ref[rows, :] = x_ref[rows, :] + update

    def norm2():
        h2 = _norm_modulate(xm_ref[rows, :], g2_ref[...], _mod_row(mod_ref, seg, 3), _mod_row(mod_ref, seg, 4))
        h2_hi = h2.astype(BF16)
        h2_ref[rows, :] = h2_hi
        h2lo_scr[...] = (h2 - h2_hi.astype(F32)).astype(BF16)

    def router():
        h2_hi = h2_ref[rows, :]
        h2_lo = h2lo_scr[...]
        logits = _dot(h2_hi, wrh_ref[...]) + (_dot(h2_lo, wrh_ref[...]) + _dot(h2_hi, wrl_ref[...]))
        lane = lax.broadcasted_iota(jnp.int32, logits.shape, 1)
        logits = jnp.where(lane < N_EXPERTS, logits, NEG_INF)
        e = jnp.exp(logits - jnp.max(logits, axis=-1, keepdims=True))
        aff = e / jnp.sum(e, axis=-1, keepdims=True)
        afft_ref[:, rows] = aff.T[:N_EXPERTS, :]

    return [norm1] + [branch(n) for n in range(N_BRANCH)] + [project, norm2, router]


def _mixmerge_body(x_ref, attc_ref, attl_ref, a_ref, g_ref, p_ref, u_ref, v_ref,
                   ap_ref, gp_ref, pp_ref, an_ref, gn_ref, pn_ref,
                   cw_ref, cb_ref, clg_ref, clb_ref, pw_ref, ps_ref, slg_ref, slb_ref, sw_ref, sbt_ref,
                   a2_ref, g2c_ref, p2_ref, u2_ref, v2_ref, an2_ref, gn2_ref, pn2_ref,
                   mod_ref, g1_ref, g2_ref, wg_ref, wb_ref, wo_ref, wrh_ref, wrl_ref,
                   xm_ref, h2_ref, afft_ref,
                   ypad, zpad, yshift, acc_scr, ug_scr, vn_scr, h_scr, merged_scr, h2lo_scr, br_cur, br_next):
    i = pl.program_id(0)
    tm = 2 * CHUNK
    seg = _segment_of_row(i * tm)
    is_ctx = i < N_CTX // tm
    mix = (cw_ref, cb_ref, clg_ref, clb_ref, pw_ref, ps_ref, slg_ref, slb_ref, sw_ref, sbt_ref)
    mix_work = (ypad, zpad, yshift, acc_scr, ug_scr, vn_scr)
    merge_refs = (x_ref, mod_ref, g1_ref, g2_ref, wg_ref, wb_ref, wo_ref, wrh_ref, wrl_ref,
                  xm_ref, h2_ref, afft_ref)
    merge_work = (h_scr, merged_scr, h2lo_scr)
    first, second = slice(0, CHUNK), slice(CHUNK, 2 * CHUNK)
    mid_prev, mid_next = slice(CHUNK - HALO, CHUNK), slice(CHUNK, CHUNK + HALO)
    tail = slice(2 * CHUNK - HALO, 2 * CHUNK)

    def tile_rows(rows):
        return {"a": lambda: a_ref[rows, :], "g": lambda: g_ref[rows, :], "p": lambda: p_ref[rows, :],
                "u": lambda: u_ref[rows, :], "v": lambda: v_ref[rows, :]}

    def halo(refs, rows=slice(None)):
        return lambda: tuple(r[rows, :] for r in refs)

    tile_agp = (a_ref, g_ref, p_ref)
    src0 = dict(tile_rows(first), prev=halo((ap_ref, gp_ref, pp_ref)), nxt=halo(tile_agp, mid_next))
    src1 = dict(tile_rows(second), prev=halo(tile_agp, mid_prev), nxt=halo((an_ref, gn_ref, pn_ref)))
    src2 = {"a": lambda: a2_ref[...], "g": lambda: g2c_ref[...], "p": lambda: p2_ref[...],
            "u": lambda: u2_ref[...], "v": lambda: v2_ref[...],
            "prev": halo(tile_agp, tail), "nxt": halo((an2_ref, gn2_ref, pn2_ref))}

    @pl.when(i == 0)
    def _():
        for step in _mixers_steps(0, src0, mix, mix_work, br_next):
            step()

    def attention(rows):
        return lambda: jnp.where(is_ctx, attc_ref[rows, :], attl_ref[rows, :])

    _interleave(_merge_steps(first, seg, attention(first), br_next, merge_refs, merge_work),
                _mixers_steps(2 * i + 1, src1, mix, mix_work, br_cur))
    _interleave(_merge_steps(second, seg, attention(second), br_cur, merge_refs, merge_work),
                _mixers_steps(2 * i + 2, src2, mix, mix_work, br_next))


def _mixmerge(x, att_ctx, att_lat, proj, mod, norm1_g, norm2_g, conv_w, conv_b, conv_ln_g, conv_ln_b,
              pool_w_bf, pool_scale, sgu_ln_g, sgu_ln_b, sgu_w_bf, sgu_bt, w_gate_half, w_branch_bf, w_out_half,
              w_router_hi, w_router_lo, l):
    tm = 512
    n_ctx_tiles = N_CTX // tm
    n_lat_tiles = N_LAT // tm
    halos_per_tile = tm // HALO
    n_halo = N_TOK // HALO

    def rows(w):
        return pl.BlockSpec((tm, w), lambda i: (i, 0))

    def main(col):
        return pl.BlockSpec((tm, COL_BLK), lambda i: (i, col))

    def prev(col):
        return pl.BlockSpec((HALO, COL_BLK), lambda i: (jnp.maximum(i * halos_per_tile - 1, 0), col))

    def nxt(col):
        return pl.BlockSpec((HALO, COL_BLK), lambda i: (jnp.minimum((i + 1) * halos_per_tile, n_halo - 1), col))

    def ahead(col):
        return pl.BlockSpec((CHUNK, COL_BLK), lambda i: (jnp.minimum(2 * i + 2, N_TOK // CHUNK - 1), col))

    def ahead_nxt(col):
        return pl.BlockSpec(
            (HALO, COL_BLK),
            lambda i: (jnp.minimum((i + 1) * halos_per_tile + CHUNK // HALO, n_halo - 1), col))

    def vec(n):
        return pl.BlockSpec((None, 1, n), lambda i: (l, 0, 0))

    def const(shape, index_map):
        return pl.BlockSpec(shape, index_map, pipeline_mode=pl.Buffered(1))

    return pl.pallas_call(
        _mixmerge_body,
        grid=(N_TOK // tm,),
        in_specs=[
            rows(D_MODEL),
            pl.BlockSpec((tm, W_ATT), lambda i: (jnp.minimum(i, n_ctx_tiles - 1), 0)),
            pl.BlockSpec((tm, W_ATT), lambda i: (jnp.clip(i - n_ctx_tiles, 0, n_lat_tiles - 1), 0)),
            main(3), main(4), main(5), main(6), main(7),
            prev(3), prev(4), prev(5), nxt(3), nxt(4), nxt(5),
            pl.BlockSpec((None, CONV_K, W_CONV), lambda i: (l, 0, 0)),
            vec(W_CONV), vec(W_CONV), vec(W_CONV),
            pl.BlockSpec((None, len(POOL_WINDOWS), POOL_GROUP, POOL_GROUP), lambda i: (l, 0, 0, 0)),
            vec(W_POOL), vec(W_SGU), vec(W_SGU),
            pl.BlockSpec((None, SGU_GROUPS, SGU_CHUNK, SGU_CHUNK), lambda i: (l, 0, 0, 0)),
            pl.BlockSpec((None, SGU_CHUNK, SGU_GROUPS), lambda i: (l, 0, 0)),
            ahead(3), ahead(4), ahead(5), ahead(6), ahead(7), ahead_nxt(3), ahead_nxt(4), ahead_nxt(5),
            pl.BlockSpec((None, 8, 6 * D_MODEL), lambda i: (l, 0, 0)),
            vec(D_MODEL), vec(D_MODEL),
            const((None, D_MODEL, N_BRANCH * D_MODEL), lambda i: (l, 0, 0)),
            const((None, N_BRANCH, BRANCH_W, D_MODEL), lambda i: (l, 0, 0, 0)),
            const((None, D_MODEL, D_MODEL), lambda i: (l, 0, 0)),
            const((None, D_MODEL, LANES), lambda i: (l, 0, 0)),
            const((None, D_MODEL, LANES), lambda i: (l, 0, 0)),
        ],
        out_specs=[rows(D_MODEL), rows(D_MODEL), pl.BlockSpec((N_EXPERTS, tm), lambda i: (0, i))],
        out_shape=[
            jax.ShapeDtypeStruct((N_TOK, D_MODEL), F32),
            jax.ShapeDtypeStruct((N_TOK, D_MODEL), BF16),
            jax.ShapeDtypeStruct((N_EXPERTS, N_TOK), F32),
        ],
        scratch_shapes=[
            pltpu.VMEM((CHUNK + 2 * HALO, W_CONV), F32),
            pltpu.VMEM((CHUNK + 2 * HALO, W_POOL), F32),
            pltpu.VMEM((SUBLANES, CHUNK + 2 * HALO - SUBLANES, W_CONV), F32),
            pltpu.VMEM((CHUNK, W_CONV), F32),
            pltpu.VMEM((CHUNK, W_SGU), F32),
            pltpu.VMEM((CHUNK, W_SGU), BF16),
            pltpu.VMEM((CHUNK, D_MODEL), BF16),
            pltpu.VMEM((CHUNK, D_MODEL), F32),
            pltpu.VMEM((CHUNK, D_MODEL), BF16),
            pltpu.VMEM((3, CHUNK, BRANCH_W), BF16),
            pltpu.VMEM((3, CHUNK, BRANCH_W), BF16),
        ],
        compiler_params=_cparams(1, 56),
        name="mixmerge",
    )(x, att_ctx, att_lat, *([proj] * 11), conv_w, conv_b, conv_ln_g, conv_ln_b, pool_w_bf, pool_scale,
      sgu_ln_g, sgu_ln_b, sgu_w_bf, sgu_bt, *([proj] * 8), mod, norm1_g, norm2_g, w_gate_half, w_branch_bf,
      w_out_half, w_router_hi, w_router_lo)


def _route_body(afft_ref, tri_ref, pos_ref, gate_ref, *, n_seq, seq_len, cap):
    a = jnp.concatenate([afft_ref[:, s * seq_len:(s + 1) * seq_len] for s in range(n_seq)], axis=0)
    n_rows = n_seq * N_EXPERTS
    capf = float(cap)

    def count(mask):
        return jnp.sum(jnp.where(mask, 1.0, 0.0), axis=1, keepdims=True)

    def as_float(bits):
        return lax.bitcast_convert_type(bits, F32)

    thr = jnp.zeros((n_rows, 1), jnp.int32)
    for bit in range(30, -1, -1):
        cand = thr | (1 << bit)
        thr = jnp.where(count(a >= as_float(cand)) >= capf, cand, thr)
    above = a >= as_float(thr + 1)
    tie = jnp.logical_and(a >= as_float(thr), jnp.logical_not(above))
    need = capf - count(above)

    idx = lax.broadcasted_iota(jnp.int32, (n_rows, seq_len), 1)
    last = jnp.zeros((n_rows, 1), jnp.int32)
    for bit in range(seq_len.bit_length() - 2, -1, -1):
        cand = last | (1 << bit)
        before = jnp.sum(jnp.where(tie, jnp.where(idx < cand, 1.0, 0.0), 0.0), axis=1, keepdims=True)
        last = jnp.where(before < need, cand, last)
    keep = jnp.where(above, 1.0, jnp.where(tie, jnp.where(idx <= last, 1.0, 0.0), 0.0))

    blk = tri_ref.shape[0]
    offset = jnp.zeros((n_rows, 1), F32)
    pos = []
    for c in range(seq_len // blk):
        kb = keep[:, c * blk:(c + 1) * blk]
        pos.append(_dot(kb.astype(BF16), tri_ref[...]) + offset)
        offset = offset + jnp.sum(kb, axis=1, keepdims=True)
    pos = jnp.concatenate(pos, axis=1)
    pos_ref[...] = jnp.where(keep > 0.0, pos, -1.0)
    gate_ref[...] = a


def _route(aff_t, tri, n_seq, seq_len, cap, col0):
    n_rows = n_seq * N_EXPERTS
    width = n_seq * seq_len
    shape = jax.ShapeDtypeStruct((n_rows, seq_len), F32)
    spec = pl.BlockSpec((n_rows, seq_len), lambda i: (0, 0))
    return pl.pallas_call(
        functools.partial(_route_body, n_seq=n_seq, seq_len=seq_len, cap=cap),
        grid=(1,),
        in_specs=[
            pl.BlockSpec((N_EXPERTS, width), lambda i: (0, col0 // width)),
            pl.BlockSpec(tri.shape, lambda i: (0, 0)),
        ],
        out_specs=[spec, spec],
        out_shape=[shape, shape],
        compiler_params=_cparams(1, 32),
        name="route_ctx" if cap == CAP_CTX else "route_lat",
    )(aff_t, tri)


def _slot_onehot(pos_ref, cap, values_ref=None, seq=0):
    n_tok = pos_ref.shape[1]
    slot = lax.broadcasted_iota(jnp.int32, (cap, n_tok), 0).astype(F32)
    rows = []
    for e in range(seq * N_EXPERTS, (seq + 1) * N_EXPERTS):
        hit = pos_ref[e:e + 1, :] == slot
        val = 1.0 if values_ref is None else values_ref[e:e + 1, :]
        rows.append(jnp.where(hit, val, 0.0).astype(BF16))
    return jnp.concatenate(rows, axis=0)


def _gather_ctx_body(pos_ref, h_ref, o_ref):
    for s in range(MOE_SEQS):
        onehot = _slot_onehot(pos_ref, CAP_CTX, seq=s)
        picked = _dot(onehot, h_ref[s * SEQ:(s + 1) * SEQ, :]).astype(BF16)
        o_ref[:, s * CAP_CTX:(s + 1) * CAP_CTX, :] = picked.reshape(N_EXPERTS, CAP_CTX, D_MODEL)


def _gather_ctx(pos, h2):
    return pl.pallas_call(
        _gather_ctx_body,
        grid=(BATCH // MOE_SEQS,),
        in_specs=[
            pl.BlockSpec((MOE_SEQS * N_EXPERTS, SEQ), lambda s: (s, 0)),
            pl.BlockSpec((MOE_SEQS * SEQ, D_MODEL), lambda s: (s, 0)),
        ],
        out_specs=pl.BlockSpec((N_EXPERTS, MOE_SEQS * CAP_CTX, D_MODEL), lambda s: (0, s, 0)),
        out_shape=jax.ShapeDtypeStruct((N_EXPERTS, BATCH * CAP_CTX, D_MODEL), BF16),
        compiler_params=_cparams(1, 32),
        name="gather_ctx",
    )(pos, h2)


def _gather_lat_body(pos_ref, h_ref, o_ref):
    e = pl.program_id(1)
    slot = lax.broadcasted_iota(jnp.int32, (CAP_LAT, DEC_SEQ), 0).astype(F32)
    onehot = jnp.where(pos_ref[pl.ds(e, 1), :] == slot, 1.0, 0.0).astype(BF16)
    o_ref[...] = _dot(onehot, h_ref[...]).astype(BF16)


def _gather_lat(pos, h2):
    blk0 = N_CTX // DEC_SEQ
    return pl.pallas_call(
        _gather_lat_body,
        grid=(DEC_BATCH, N_EXPERTS),
        in_specs=[
            pl.BlockSpec((N_EXPERTS, DEC_SEQ), lambda b, e: (b, 0)),
            pl.BlockSpec((DEC_SEQ, D_MODEL), lambda b, e: (blk0 + b, 0)),
        ],
        out_specs=pl.BlockSpec((None, CAP_LAT, D_MODEL), lambda b, e: (e, b, 0)),
        out_shape=jax.ShapeDtypeStruct((N_EXPERTS, DEC_BATCH * CAP_LAT, D_MODEL), BF16),
        compiler_params=_cparams(2, 40),
        name="gather_lat",
    )(pos, h2)


def _ffn_body(xc_ref, xl_ref, w1_ref, w3_ref, w2_ref, yc_ref, yl_ref):
    w1 = w1_ref[...].astype(BF16)
    w3 = w3_ref[...].astype(BF16)
    w2 = w2_ref[...].astype(BF16)
    rows = 512
    for x_ref, y_ref in ((xc_ref, yc_ref), (xl_ref, yl_ref)):
        for r0 in range(0, x_ref.shape[0], rows):
            x = x_ref[r0:r0 + rows, :]
            a = _dot(x, w1)
            act = (a * _sigmoid(a) * _dot(x, w3)).astype(BF16)
            y_ref[r0:r0 + rows, :] = _dot(act, w2).astype(BF16)


def _ffn(xg_ctx, xg_lat, w1, w3, w2, l):
    n_c, n_l = xg_ctx.shape[1], xg_lat.shape[1]

    def wspec():
        return pl.BlockSpec((None, None, D_MODEL, D_MODEL), lambda e: (l, e, 0, 0))

    def xspec(n):
        return pl.BlockSpec((None, n, D_MODEL), lambda e: (e, 0, 0))

    return pl.pallas_call(
        _ffn_body,
        grid=(N_EXPERTS,),
        in_specs=[xspec(n_c), xspec(n_l), wspec(), wspec(), wspec()],
        out_specs=[xspec(n_c), xspec(n_l)],
        out_shape=[jax.ShapeDtypeStruct(xg_ctx.shape, BF16), jax.ShapeDtypeStruct(xg_lat.shape, BF16)],
        compiler_params=_cparams(1, 56),
        name="ffn",
    )(xg_ctx, xg_lat, w1, w3, w2)


def _combine_body(pos_ref, gate_ref, y_ref, x_ref, mod_ref, *rest, cap, seg0, n_seq):
    o_ref = rest[-1]
    seg = seg0 + pl.program_id(0) if seg0 else 0
    n_tok = pos_ref.shape[1]
    for s in range(n_seq):
        rows = slice(s * n_tok, (s + 1) * n_tok)
        weights = _slot_onehot(pos_ref, cap, gate_ref, seq=s)
        y = y_ref[:, s * cap:(s + 1) * cap, :].reshape(N_EXPERTS * cap, D_MODEL)
        out = x_ref[rows, :] + _mod_row(mod_ref, seg, 5) * _dot_tn(weights, y)
        if len(rest) == 2:
            out = out * lax.rsqrt(jnp.mean(out * out, axis=-1, keepdims=True) + EPS) * rest[0][...]
        o_ref[rows, :] = out


def _combine_ctx(pos, gate, y, x_mid, mod, l, final_g=None):
    last = final_g is not None
    return pl.pallas_call(
        functools.partial(_combine_body, cap=CAP_CTX, seg0=0, n_seq=MOE_SEQS),
        grid=(BATCH // MOE_SEQS,),
        in_specs=[
            pl.BlockSpec((MOE_SEQS * N_EXPERTS, SEQ), lambda s: (s, 0)),
            pl.BlockSpec((MOE_SEQS * N_EXPERTS, SEQ), lambda s: (s, 0)),
            pl.BlockSpec((N_EXPERTS, MOE_SEQS * CAP_CTX, D_MODEL), lambda s: (0, s, 0)),
            pl.BlockSpec((MOE_SEQS * SEQ, D_MODEL), lambda s: (s, 0)),
            pl.BlockSpec((None, 8, 6 * D_MODEL), lambda s: (l, 0, 0)),
        ] + ([pl.BlockSpec((1, D_MODEL), lambda s: (0, 0))] if last else []),
        out_specs=pl.BlockSpec((MOE_SEQS * SEQ, D_MODEL), lambda s: (s, 0)),
        out_shape=jax.ShapeDtypeStruct((N_CTX if last else N_TOK, D_MODEL), F32),
        input_output_aliases={} if last else {3: 0},
        compiler_params=_cparams(1, 32),
        name="combine_ctx",
    )(pos, gate, y, x_mid, mod, *([final_g] if last else []))


def _combine_lat(pos, gate, y, x_mid, mod, l, final_g=None):
    last = final_g is not None
    tiles = DEC_SEQ // CHUNK
    blk0 = N_CTX // CHUNK
    return pl.pallas_call(
        functools.partial(_combine_body, cap=CAP_LAT, seg0=1, n_seq=1),
        grid=(DEC_BATCH, tiles),
        in_specs=[
            pl.BlockSpec((N_EXPERTS, CHUNK), lambda b, t: (b, t)),
            pl.BlockSpec((N_EXPERTS, CHUNK), lambda b, t: (b, t)),
            pl.BlockSpec((N_EXPERTS, CAP_LAT, D_MODEL), lambda b, t: (0, b, 0)),
            pl.BlockSpec((CHUNK, D_MODEL), lambda b, t: (blk0 + b * tiles + t, 0)),
            pl.BlockSpec((None, 8, 6 * D_MODEL), lambda b, t: (l, 0, 0)),
        ] + ([pl.BlockSpec((1, D_MODEL), lambda b, t: (0, 0))] if last else []),
        out_specs=pl.BlockSpec((CHUNK, D_MODEL), lambda b, t: ((0 if last else blk0) + b * tiles + t, 0)),
        out_shape=jax.ShapeDtypeStruct((N_LAT if last else N_TOK, D_MODEL), F32),
        input_output_aliases={} if last else {3: 0},
        compiler_params=_cparams(2, 48),
        name="combine_lat",
    )(pos, gate, y, x_mid, mod, *([final_g] if last else []))


def kernel(x_prompt, x_sample, cache_k, cache_v, c, c_ctx, w_ada, b_ada, norm1_g, w_in, rpb, conv_w, conv_b,
           conv_ln_g, conv_ln_b, pool_w, pool_scale, sgu_ln_g, sgu_ln_b, sgu_w, sgu_b, w_branch, w_out,
           norm2_g, w_router, w1, w3, w2, final_g):
    def vecs(a):
        return a.reshape(DEPTH, 1, a.shape[-1])

    cond8 = jnp.concatenate([c_ctx[None, :], c, jnp.zeros((8 - 1 - DEC_BATCH, D_MODEL), F32)], axis=0)
    mod = _adaln(cond8, w_ada, b_ada)
    table = _rpb_table(rpb)
    cos_np, sin_np = _rope_tables()
    cos, sin = jnp.asarray(cos_np), jnp.asarray(sin_np)
    tri = jnp.asarray(np.triu(np.ones((CHUNK, CHUNK), np.float32), 1), BF16)

    w_gate_half = (0.5 * w_in[:, :, D_PROJ:]).astype(BF16)
    w_branch_bf = w_branch.astype(BF16)
    w_out_half = (0.5 * w_out).astype(BF16)
    pool_w_bf = pool_w.astype(BF16)
    sgu_w_bf = sgu_w.astype(BF16)
    sgu_bt = sgu_b.transpose(0, 2, 1)
    w_router_pad = jnp.pad(w_router, ((0, 0), (0, 0), (0, LANES - N_EXPERTS)))
    w_router_hi = w_router_pad.astype(BF16)
    w_router_lo = (w_router_pad - w_router_hi.astype(F32)).astype(BF16)
    norm1_v, norm2_v = vecs(norm1_g), vecs(norm2_g)
    conv_b_v, conv_ln_g_v, conv_ln_b_v = vecs(conv_b), vecs(conv_ln_g), vecs(conv_ln_b)
    pool_scale_v, sgu_ln_g_v, sgu_ln_b_v = vecs(pool_scale), vecs(sgu_ln_g), vecs(sgu_ln_b)

    x = jnp.concatenate([x_prompt.reshape(N_CTX, D_MODEL), x_sample.reshape(N_LAT, D_MODEL)], axis=0)
    projs = []
    for l in range(DEPTH):
        proj = _proj(x, norm1_v, mod, w_in, l)
        projs.append(proj)
        att_ctx = _attn_ctx(proj)
        att_lat = _attn_lat(proj, cache_k, cache_v, table, cos, sin, l)
        x_mid, h2, aff_t = _mixmerge(x, att_ctx, att_lat, proj, mod, norm1_v, norm2_v, conv_w, conv_b_v,
                                     conv_ln_g_v, conv_ln_b_v, pool_w_bf, pool_scale_v, sgu_ln_g_v, sgu_ln_b_v,
                                     sgu_w_bf, sgu_bt, w_gate_half, w_branch_bf, w_out_half, w_router_hi,
                                     w_router_lo, l)
        pos_c, gate_c = _route(aff_t, tri, BATCH, SEQ, CAP_CTX, 0)
        pos_l, gate_l = _route(aff_t, tri, DEC_BATCH, DEC_SEQ, CAP_LAT, N_CTX)
        y_c, y_l = _ffn(_gather_ctx(pos_c, h2), _gather_lat(pos_l, h2), w1, w3, w2, l)
        if l < DEPTH - 1:
            x = _combine_ctx(pos_c, gate_c, y_c, x_mid, mod, l)
            x = _combine_lat(pos_l, gate_l, y_l, x, mod, l)
        else:
            y_prompt = _combine_ctx(pos_c, gate_c, y_c, x_mid, mod, l, final_g[None, :])
            y_sample = _combine_lat(pos_l, gate_l, y_l, x_mid, mod, l, final_g[None, :])

    y_prompt = y_prompt.reshape(BATCH, SEQ, D_MODEL)
    y_sample = y_sample.reshape(DEC_BATCH, DEC_SEQ, D_MODEL)
    new_k, new_v = _kv_out(projs)
    return (y_prompt, y_sample, new_k, new_v)
```

```python
import functools

import numpy as np
import jax
import jax.numpy as jnp
from jax import lax
from jax.experimental import pallas as pl
from jax.experimental.pallas import tpu as pltpu

F32 = jnp.float32
BF16 = jnp.bfloat16

D_MODEL = 1024
BATCH = 32
SEQ = 256
DEPTH = 4
DEC_BATCH = 2
DEC_SEQ = 2048
PAST_LEN = 512
GRID_W = 64
GRID_ROWS = DEC_SEQ // GRID_W
N_HEADS = 8
HEAD_DIM = 64
W_ATT = N_HEADS * HEAD_DIM
NA_ROWS = 8
NA_COLS = 16
ROPE_BASE = 10000.0
W_CONV = 512
CONV_K = 31
W_POOL = 512
POOL_WINDOWS = (2, 4, 8, 16)
POOL_GROUP = 128
W_SGU = 512
SGU_GROUPS = 4
SGU_CHUNK = 128
N_BRANCH = 4
BRANCH_W = 512
N_EXPERTS = 16
EC_CAPACITY = 2
EPS = 1e-6
NEG_INF = -1e30

N_CTX = BATCH * SEQ
N_LAT = DEC_BATCH * DEC_SEQ
N_TOK = N_CTX + N_LAT
D_PROJ = 4096
COL_BLK = 512
CAP_CTX = EC_CAPACITY * SEQ // N_EXPERTS
CAP_LAT = EC_CAPACITY * DEC_SEQ // N_EXPERTS
N_ROW_OFF = 2 * NA_ROWS
Q_ROWS = 4
KEY_ROWS = 12
LAT_EXPERTS = 4
LAT_TOKENS = 512
MOE_SEQS = 4
KV_SEQS = 4
HEAD_GROUP = 4
HALO = 16
CHUNK = 256
LANES = 128
SUBLANES = 8
VMEM_LIMIT_CAP = 56 * 1024 * 1024

HIGHEST = lax.Precision.HIGHEST


def _cparams(n_grid, vmem_mb):
    return pltpu.CompilerParams(
        dimension_semantics=("arbitrary",) * n_grid,
        vmem_limit_bytes=min(vmem_mb * 1024 * 1024, VMEM_LIMIT_CAP))


def _dot(a, b):
    return jnp.dot(a, b, preferred_element_type=F32)


def _dot_tn(a, b):
    return lax.dot_general(a, b, (((0,), (0,)), ((), ())), preferred_element_type=F32)


def _sigmoid(x):
    return 0.5 * jnp.tanh(0.5 * x) + 0.5


def _segment_of_row(row0):
    return jnp.where(row0 < N_CTX, 0, 1 + (row0 - N_CTX) // DEC_SEQ)


def _mod_row(mod_ref, seg, k):
    return mod_ref[pl.ds(seg, 1), k * D_MODEL:(k + 1) * D_MODEL]


def _norm_modulate(x, g, shift, scale):
    y = x * lax.rsqrt(jnp.mean(x * x, axis=-1, keepdims=True) + EPS) * g
    return y * (1.0 + scale) + shift


def _layernorm(x, g, b):
    mu = jnp.mean(x, axis=-1, keepdims=True)
    xc = x - mu
    var = jnp.mean(xc * xc, axis=-1, keepdims=True)
    return xc * lax.rsqrt(var + EPS) * g + b


def _adaln_body(cond_ref, w_ref, b_ref, o_ref):
    c = cond_ref[...]
    a = (c * _sigmoid(c)).astype(BF16)
    o_ref[...] = _dot(a, w_ref[...].astype(BF16)) + b_ref[...]


def _adaln(cond8, w_ada, b_ada):
    tn = 1536
    n6 = 6 * D_MODEL
    return pl.pallas_call(
        _adaln_body,
        grid=(DEPTH, n6 // tn),
        in_specs=[
            pl.BlockSpec((8, D_MODEL), lambda l, j: (0, 0)),
            pl.BlockSpec((None, D_MODEL, tn), lambda l, j: (l, 0, j)),
            pl.BlockSpec((None, 1, tn), lambda l, j: (l, 0, j)),
        ],
        out_specs=pl.BlockSpec((None, 8, tn), lambda l, j: (l, 0, j)),
        out_shape=jax.ShapeDtypeStruct((DEPTH, 8, n6), F32),
        compiler_params=_cparams(2, 32),
        name="adaln",
    )(cond8, w_ada, b_ada.reshape(DEPTH, 1, n6))


def _rpb_onehots():
    cq = np.arange(GRID_W)[None, :]
    kc = np.arange(GRID_W)[:, None]
    start = np.clip(cq - NA_COLS // 2, 0, GRID_W - NA_COLS)
    valid = (kc >= start) & (kc < start + NA_COLS)
    d_col = np.clip(kc - cq, -(NA_COLS - 1), NA_COLS - 1) + (NA_COLS - 1)
    left = np.zeros((32, GRID_W, 2 * GRID_W), np.float32)
    right = np.zeros((32, GRID_W, 2 * GRID_W), np.float32)
    mask = np.zeros((GRID_W, 2 * GRID_W), np.float32)
    for j in range(2 * NA_COLS - 1):
        hit = (d_col == j) & valid
        left[j, :, :GRID_W] = hit
        right[j, :, GRID_W:] = hit
    mask[:, :GRID_W] = np.where(valid, 0.0, NEG_INF)
    mask[:, GRID_W:] = np.where(valid, 0.0, NEG_INF)
    n = GRID_W * 2 * GRID_W
    return left.reshape(32, n), right.reshape(32, n), mask.reshape(1, n)


def _rpb_body(a_ref, b_ref, left_ref, right_ref, mask_ref, o_ref):
    t = jnp.dot(a_ref[...], left_ref[...], precision=HIGHEST, preferred_element_type=F32)
    t = t + jnp.dot(b_ref[...], right_ref[...], precision=HIGHEST, preferred_element_type=F32)
    o_ref[...] = t + mask_ref[...]


def _rpb_table(rpb):
    left, right, mask = _rpb_onehots()
    n_rows = N_HEADS * N_ROW_OFF
    rpb_p = jnp.pad(rpb, ((0, 0), (0, 0), (0, 0), (0, 1)))
    a = jnp.pad(rpb_p, ((0, 0), (0, 0), (0, 1), (0, 0))).reshape(DEPTH, n_rows, 32)
    b = jnp.pad(rpb_p, ((0, 0), (0, 0), (1, 0), (0, 0))).reshape(DEPTH, n_rows, 32)
    n = left.shape[1]
    out = pl.pallas_call(
        _rpb_body,
        grid=(DEPTH,),
        in_specs=[
            pl.BlockSpec((None, n_rows, 32), lambda l: (l, 0, 0)),
            pl.BlockSpec((None, n_rows, 32), lambda l: (l, 0, 0)),
            pl.BlockSpec((32, n), lambda l: (0, 0)),
            pl.BlockSpec((32, n), lambda l: (0, 0)),
            pl.BlockSpec((1, n), lambda l: (0, 0)),
        ],
        out_specs=pl.BlockSpec((None, n_rows, n), lambda l: (l, 0, 0)),
        out_shape=jax.ShapeDtypeStruct((DEPTH, n_rows, n), F32),
        compiler_params=_cparams(1, 32),
        name="rpb_table",
    )(a, b, jnp.asarray(left), jnp.asarray(right), jnp.asarray(mask))
    return out.reshape(DEPTH, N_HEADS, N_ROW_OFF, GRID_W, 2 * GRID_W)


def _proj_body(x_ref, g_ref, mod_ref, w_ref, o_ref, h_scr, *, tm):
    i = pl.program_id(0)

    @pl.when(pl.program_id(1) == 0)
    def _():
        seg = _segment_of_row(i * tm)
        h = _norm_modulate(x_ref[...], g_ref[...], _mod_row(mod_ref, seg, 0), _mod_row(mod_ref, seg, 1))
        h_scr[...] = h.astype(BF16)

    o_ref[...] = _dot(h_scr[...], w_ref[...]).astype(BF16)


def _proj(x, norm_g, mod, w_proj_bf, l):
    tm, tn = 1024, 2048
    return pl.pallas_call(
        functools.partial(_proj_body, tm=tm),
        grid=(N_TOK // tm, D_PROJ // tn),
        in_specs=[
            pl.BlockSpec((tm, D_MODEL), lambda i, j: (i, 0)),
            pl.BlockSpec((None, 1, D_MODEL), lambda i, j: (l, 0, 0)),
            pl.BlockSpec((None, 8, 6 * D_MODEL), lambda i, j: (l, 0, 0)),
            pl.BlockSpec((None, D_MODEL, tn), lambda i, j: (l, 0, j)),
        ],
        out_specs=pl.BlockSpec((tm, tn), lambda i, j: (i, j)),
        out_shape=jax.ShapeDtypeStruct((N_TOK, D_PROJ), BF16),
        scratch_shapes=[pltpu.VMEM((tm, D_MODEL), BF16)],
        compiler_params=_cparams(2, 48),
        name="proj",
    )(x, norm_g, mod, w_proj_bf)


def _head_rows_in_pair(x_t, h):
    rows = x_t[h * HEAD_DIM:(h + 1) * HEAD_DIM, :]
    zeros = jnp.zeros_like(rows)
    return jnp.concatenate([rows, zeros] if h % 2 == 0 else [zeros, rows], axis=0)


def _pair_lanes(h):
    return slice((h // 2) * LANES, (h // 2 + 1) * LANES)


def _attn_ctx_body(q_ref, k_ref, v_ref, o_ref):
    k = k_ref[...]
    v = v_ref[...]
    q_t = (q_ref[...].astype(F32) * (HEAD_DIM ** -0.5)).T.astype(BF16)
    v_t = v.astype(F32).T.astype(BF16)
    heads = range(N_HEADS)
    s_t = [_dot(k[:, _pair_lanes(h)], _head_rows_in_pair(q_t, h)) for h in heads]
    e_t = [jnp.exp(s - jnp.max(s, axis=0, keepdims=True)) for s in s_t]
    denom = [jnp.sum(e, axis=0, keepdims=True) for e in e_t]
    o_t = [_dot(v_t[h * HEAD_DIM:(h + 1) * HEAD_DIM, :], e_t[h].astype(BF16)) / denom[h] for h in heads]
    o_ref[...] = jnp.concatenate(o_t, axis=0).T.astype(BF16)


def _attn_ctx(proj):
    return pl.pallas_call(
        _attn_ctx_body,
        grid=(BATCH,),
        in_specs=[
            pl.BlockSpec((SEQ, COL_BLK), lambda b: (b, 0)),
            pl.BlockSpec((SEQ, COL_BLK), lambda b: (b, 1)),
            pl.BlockSpec((SEQ, COL_BLK), lambda b: (b, 2)),
        ],
        out_specs=pl.BlockSpec((SEQ, W_ATT), lambda b: (b, 0)),
        out_shape=jax.ShapeDtypeStruct((N_CTX, W_ATT), BF16),
        compiler_params=_cparams(1, 32),
        name="attn_ctx",
    )(proj, proj, proj)


def _kv_out_body(*refs):
    kv_refs, (ck_ref, cv_ref) = refs[:2 * DEPTH], refs[2 * DEPTH:]
    l = pl.program_id(0)
    for j in range(DEPTH):
        @pl.when(l == j)
        def _(j=j):
            for s in range(KV_SEQS):
                rows = slice(s * SEQ, (s + 1) * SEQ)
                k = kv_refs[2 * j][rows, :]
                v = kv_refs[2 * j + 1][rows, :]
                for h in range(N_HEADS):
                    sl = slice(h * HEAD_DIM, (h + 1) * HEAD_DIM)
                    ck_ref[s, h] = k[:, sl].astype(F32)
                    cv_ref[s, h] = v[:, sl].astype(F32)


def _kv_out(projs):
    steps = BATCH // KV_SEQS

    def src(j, col):
        return pl.BlockSpec((KV_SEQS * SEQ, COL_BLK),
                            lambda l, b: (jnp.where(l == j, b, jnp.where(l < j, 0, steps - 1)), col))

    cache_spec = pl.BlockSpec((KV_SEQS, None, N_HEADS, SEQ, HEAD_DIM), lambda l, b: (b, l, 0, 0, 0))
    cache_shape = jax.ShapeDtypeStruct((BATCH, DEPTH, N_HEADS, SEQ, HEAD_DIM), F32)
    in_specs, args = [], []
    for j in range(DEPTH):
        in_specs += [src(j, 1), src(j, 2)]
        args += [projs[j], projs[j]]
    return pl.pallas_call(
        _kv_out_body,
        grid=(DEPTH, steps),
        in_specs=in_specs,
        out_specs=[cache_spec, cache_spec],
        out_shape=[cache_shape, cache_shape],
        compiler_params=_cparams(2, 48),
        name="kv_out",
    )(*args)


def _rope_tables():
    t = np.arange(DEC_SEQ)
    half = HEAD_DIM // 2
    nf = half // 2
    inv = (1.0 / (ROPE_BASE ** (np.arange(nf) / nf))).astype(np.float32)
    cos = np.zeros((DEC_SEQ, HEAD_DIM), np.float32)
    sin = np.zeros((DEC_SEQ, HEAD_DIM), np.float32)
    for blk, pos in enumerate((t // GRID_W, t % GRID_W)):
        ang = (pos[:, None].astype(np.float32) * inv[None, :]).astype(np.float32)
        c, s = np.cos(ang), np.sin(ang)
        cos[:, blk * half:(blk + 1) * half] = np.concatenate([c, c], axis=1)
        sin[:, blk * half:(blk + 1) * half] = np.concatenate([-s, s], axis=1)
    return np.tile(cos, (1, 2)), np.tile(sin, (1, 2))


def _rope(x, cos, sin):
    cos = jnp.concatenate([cos] * (W_ATT // LANES), axis=1)
    sin = jnp.concatenate([sin] * (W_ATT // LANES), axis=1)
    lane = lax.broadcasted_iota(jnp.int32, x.shape, 1)
    nf = HEAD_DIM // 4
    partner = jnp.where(lane % (2 * nf) < nf, pltpu.roll(x, W_ATT - nf, 1), pltpu.roll(x, nf, 1))
    return x * cos + partner * sin


def _attn_lat_body(q_ref, k_ref, v_ref, ck_ref, cv_ref, tab_ref, cos_ref, sin_ref, o_ref,
                   krot_scr, vt_scr, ck_scr, cvt_scr):
    g = pl.program_id(1)
    blk = Q_ROWS * GRID_W
    n_blk = DEC_SEQ // blk

    @pl.when(g == 0)
    def _():
        def chunk(c, carry):
            rows = pl.ds(pl.multiple_of(c * blk, blk), blk)
            kr = _rope(k_ref[rows, :].astype(F32), cos_ref[rows, :], sin_ref[rows, :])
            krot_scr[rows, :] = kr.astype(BF16)
            vt_scr[c] = v_ref[rows, :].astype(F32).T.astype(BF16)
            return carry
        lax.fori_loop(0, n_blk, chunk, 0)
        for h in range(0, N_HEADS, 2):
            ck_scr[h] = ck_ref[h].astype(BF16)
            ck_scr[h + 1] = ck_ref[h + 1].astype(BF16)
            pair_t = jnp.concatenate([cv_ref[h], cv_ref[h + 1]], axis=1).T
            cvt_scr[h] = pair_t[:HEAD_DIM, :].astype(BF16)
            cvt_scr[h + 1] = pair_t[HEAD_DIM:, :].astype(BF16)

    kh = min(NA_ROWS, GRID_ROWS)
    scale = HEAD_DIM ** -0.5
    r0 = g * Q_ROWS
    blk0 = jnp.clip(g - 1, 0, n_blk - KEY_ROWS // Q_ROWS)
    u = blk0 * Q_ROWS

    qrows = pl.ds(pl.multiple_of(g * blk, blk), blk)
    q = q_ref[...].astype(F32)
    q_rot_t = (_rope(q, cos_ref[qrows, :], sin_ref[qrows, :]) * scale).T.astype(BF16)
    q_plain_t = (q * scale).T.astype(BF16)
    k_win = krot_scr[pl.ds(pl.multiple_of(blk0 * blk, blk), KEY_ROWS * GRID_W), :]

    lane = lax.broadcasted_iota(jnp.int32, (GRID_W, 2 * GRID_W), 1)
    row_masks = []
    for kr in range(KEY_ROWS):
        pair_masks = []
        for ip in range(Q_ROWS // 2):
            m = []
            for i in (2 * ip, 2 * ip + 1):
                start = jnp.clip(r0 + i - kh // 2, 0, GRID_ROWS - kh)
                inside = jnp.logical_and(u + kr >= start, u + kr < start + kh)
                m.append(jnp.where(inside, 0.0, NEG_INF))
            pair_masks.append(jnp.where(lane < GRID_W, m[0], m[1]))
        row_masks.append(jnp.concatenate(pair_masks, axis=1))

    def head_rows(h):
        return slice(h * HEAD_DIM, (h + 1) * HEAD_DIM)

    def local_bias(h):
        bias_rows = []
        for kr in range(KEY_ROWS):
            tiles = []
            for ip in range(Q_ROWS // 2):
                d_row = u + kr - (r0 + 2 * ip) + (NA_ROWS - 1)
                tiles.append(tab_ref[h, jnp.clip(d_row, 0, N_ROW_OFF - 1)])
            bias_rows.append(jnp.concatenate(tiles, axis=1) + row_masks[kr])
        return jnp.concatenate(bias_rows, axis=0)

    outs = []
    for h0 in range(0, N_HEADS, HEAD_GROUP):
        heads = range(h0, h0 + HEAD_GROUP)
        s_loc = [_dot(k_win[:, _pair_lanes(h)], _head_rows_in_pair(q_rot_t, h)) + local_bias(h)
                 for h in heads]
        s_ctx = [_dot(ck_scr[h], q_plain_t[head_rows(h), :]) for h in heads]
        m = [jnp.maximum(jnp.max(a, axis=0, keepdims=True), jnp.max(b, axis=0, keepdims=True))
             for a, b in zip(s_loc, s_ctx)]
        e_loc = [jnp.exp(a - mm) for a, mm in zip(s_loc, m)]
        e_ctx = [jnp.exp(b - mm) for b, mm in zip(s_ctx, m)]
        denom = [jnp.sum(a, axis=0, keepdims=True) + jnp.sum(b, axis=0, keepdims=True)
                 for a, b in zip(e_loc, e_ctx)]
        for i, h in enumerate(heads):
            p_loc = e_loc[i].astype(BF16)
            o_t = _dot(cvt_scr[h], e_ctx[i].astype(BF16))
            for c in range(KEY_ROWS // Q_ROWS):
                o_t = o_t + _dot(vt_scr[blk0 + c, head_rows(h), :], p_loc[c * blk:(c + 1) * blk, :])
            outs.append(o_t / denom[i])
    o_ref[...] = jnp.concatenate(outs, axis=0).T.astype(BF16)


def _attn_lat(proj, cache_k, cache_v, table, cos, sin, l):
    blk = Q_ROWS * GRID_W
    steps = GRID_ROWS // Q_ROWS
    q_blk0 = N_CTX // blk
    kv_blk0 = N_CTX // DEC_SEQ
    cache_spec = pl.BlockSpec((None, None, N_HEADS, PAST_LEN, HEAD_DIM), lambda b, g: (b, l, 0, 0, 0))

    def const(shape, index_map):
        return pl.BlockSpec(shape, index_map, pipeline_mode=pl.Buffered(1))

    return pl.pallas_call(
        _attn_lat_body,
        grid=(DEC_BATCH, steps),
        in_specs=[
            pl.BlockSpec((blk, COL_BLK), lambda b, g: (q_blk0 + b * steps + g, 0)),
            pl.BlockSpec((DEC_SEQ, COL_BLK), lambda b, g: (kv_blk0 + b, 1)),
            pl.BlockSpec((DEC_SEQ, COL_BLK), lambda b, g: (kv_blk0 + b, 2)),
            cache_spec, cache_spec,
            const((None, N_HEADS, N_ROW_OFF, GRID_W, 2 * GRID_W), lambda b, g: (l, 0, 0, 0, 0)),
            const((DEC_SEQ, LANES), lambda b, g: (0, 0)),
            const((DEC_SEQ, LANES), lambda b, g: (0, 0)),
        ],
        out_specs=pl.BlockSpec((blk, W_ATT), lambda b, g: (b * steps + g, 0)),
        out_shape=jax.ShapeDtypeStruct((N_LAT, W_ATT), BF16),
        scratch_shapes=[
            pltpu.VMEM((DEC_SEQ, W_ATT), BF16),
            pltpu.VMEM((DEC_SEQ // blk, W_ATT, blk), BF16),
            pltpu.VMEM((N_HEADS, PAST_LEN, HEAD_DIM), BF16),
            pltpu.VMEM((N_HEADS, HEAD_DIM, PAST_LEN), BF16),
        ],
        compiler_params=_cparams(2, 48),
        name="attn_lat",
    )(proj, proj, proj, cache_k, cache_v, table, cos, sin)


def _interleave(major, minor):
    per = -(-len(minor) // len(major))
    for n, step in enumerate(major):
        step()
        for other in minor[n * per:(n + 1) * per]:
            other()


def _mixers_steps(chunk, src, mix, work, dst):
    cw_ref, cb_ref, clg_ref, clb_ref, pw_ref, ps_ref, slg_ref, slb_ref, sw_ref, sbt_ref = mix
    ypad, zpad, yshift, acc_scr, ug_scr, vn_scr = work
    n_ctx_chunks = N_CTX // CHUNK
    per_seq = DEC_SEQ // CHUNK
    j = (chunk - n_ctx_chunks) % per_seq
    is_lat = chunk >= n_ctx_chunks
    left_ok = jnp.logical_and(is_lat, j > 0)
    right_ok = jnp.logical_and(is_lat, j < per_seq - 1)
    seq_len = jnp.where(is_lat, DEC_SEQ, SEQ)
    chunk_off = jnp.where(is_lat, j * CHUNK, 0)

    def glu(a_, g_):
        return a_.astype(F32) * _sigmoid(g_.astype(F32))

    def fill():
        prev, nxt = src["prev"](), src["nxt"]()
        ypad[0:HALO, :] = jnp.where(left_ok, glu(prev[0], prev[1]), 0.0)
        ypad[HALO:HALO + CHUNK, :] = glu(src["a"](), src["g"]())
        ypad[HALO + CHUNK:, :] = jnp.where(right_ok, glu(nxt[0], nxt[1]), 0.0)
        zpad[0:HALO, :] = jnp.where(left_ok, prev[2].astype(F32), 0.0)
        zpad[HALO:HALO + CHUNK, :] = src["p"]().astype(F32)
        zpad[HALO + CHUNK:, :] = jnp.where(right_ok, nxt[2].astype(F32), 0.0)

    def shift(s):
        def run():
            yshift[s] = ypad[s:s + yshift.shape[1], :]
        return run

    conv_rows = 64
    half_k = CONV_K // 2

    def conv(rb, cg):
        def run():
            lanes = slice(cg * LANES, (cg + 1) * LANES)
            acc = jnp.zeros((conv_rows, LANES), F32)
            for k in range(CONV_K):
                start = HALO + rb * conv_rows + k - half_k
                s = start % SUBLANES
                acc = acc + cw_ref[k:k + 1, lanes] * yshift[s, start - s:start - s + conv_rows, lanes]
            acc_scr[rb * conv_rows:(rb + 1) * conv_rows, lanes] = acc
        return run

    def conv_finish():
        y = _layernorm(acc_scr[...] + cb_ref[...], clg_ref[...], clb_ref[...])
        dst[0] = (y * _sigmoid(y)).astype(BF16)

    def pool(gi, w):
        def run():
            t_seq = lax.broadcasted_iota(jnp.int32, (CHUNK, 1), 0) + chunk_off
            lanes = slice(gi * POOL_GROUP, (gi + 1) * POOL_GROUP)
            lo, hi = -(w // 2), w - w // 2 - 1
            tot = jnp.zeros((CHUNK, POOL_GROUP), F32)
            for jj in range(lo, hi + 1):
                tot = tot + zpad[HALO + jj:HALO + jj + CHUNK, lanes]
            cnt = w - jnp.maximum(0, -lo - t_seq) - jnp.maximum(0, t_seq + hi - (seq_len - 1))
            d = tot / cnt.astype(F32) - zpad[HALO:HALO + CHUNK, lanes]
            dst[1, :, lanes] = (_dot(d.astype(BF16), pw_ref[gi]) * ps_ref[:, lanes]).astype(BF16)
        return run

    def gating_inputs():
        ug_scr[...] = jax.nn.gelu(src["u"]().astype(F32))
        vn_scr[...] = _layernorm(jax.nn.gelu(src["v"]().astype(F32)), slg_ref[...], slb_ref[...]).astype(BF16)

    def gating(n):
        def run():
            pos = slice(n * SGU_CHUNK, (n + 1) * SGU_CHUNK)
            for grp in range(SGU_GROUPS):
                lanes = slice(grp * LANES, (grp + 1) * LANES)
                mixed = _dot(sw_ref[grp], vn_scr[pos, lanes]) + sbt_ref[:, grp:grp + 1]
                dst[2, pos, lanes] = (ug_scr[pos, lanes] * mixed).astype(BF16)
        return run

    steps = [fill] + [shift(s) for s in range(SUBLANES)]
    steps += [conv(rb, cg) for rb in range(CHUNK // conv_rows) for cg in range(W_CONV // LANES)]
    steps += [conv_finish] + [pool(gi, w) for gi, w in enumerate(POOL_WINDOWS)]
    steps += [gating_inputs] + [gating(n) for n in range(CHUNK // SGU_CHUNK)]
    return steps


def _merge_steps(rows, seg, attention, br_ref, merge_refs, work):
    x_ref, mod_ref, g1_ref, g2_ref, wg_ref, wb_ref, wo_ref, wrh_ref, wrl_ref, xm_ref, h2_ref, afft_ref = merge_refs
    h_scr, merged_scr, h2lo_scr = work

    def norm1():
        h_scr[...] = _norm_modulate(x_ref[rows, :], g1_ref[...], _mod_row(mod_ref, seg, 0),
                                    _mod_row(mod_ref, seg, 1)).astype(BF16)

    def branch(n):
        def run():
            half_z = _dot(h_scr[...], wg_ref[:, n * D_MODEL:(n + 1) * D_MODEL])
            gated = (jnp.tanh(half_z) + 1.0) * _dot(attention() if n == 0 else br_ref[n - 1], wb_ref[n])
            merged_scr[...] = gated if n == 0 else merged_scr[...] + gated
        return run

    def project():
        update = _mod_row(mod_ref, seg, 2) * _dot(merged_scr[...].astype(BF16), wo_ref[...])
        xm_ref[rows, :] = x_ref[rows, :] + update

    def norm2():
        h2 = _norm_modulate(xm_ref[rows, :], g2_ref[...], _mod_row(mod_ref, seg, 3), _mod_row(mod_ref, seg, 4))
        h2_hi = h2.astype(BF16)
        h2_ref[rows, :] = h2_hi
        h2lo_scr[...] = (h2 - h2_hi.astype(F32)).astype(BF16)

    def router():
        h2_hi = h2_ref[rows, :]
        h2_lo = h2lo_scr[...]
        logits = _dot(h2_hi, wrh_ref[...]) + (_dot(h2_lo, wrh_ref[...]) + _dot(h2_hi, wrl_ref[...]))
        lane = lax.broadcasted_iota(jnp.int32, logits.shape, 1)
        logits = jnp.where(lane < N_EXPERTS, logits, NEG_INF)
        e = jnp.exp(logits - jnp.max(logits, axis=-1, keepdims=True))
        aff = e / jnp.sum(e, axis=-1, keepdims=True)
        afft_ref[:, rows] = aff.T[:N_EXPERTS, :]

    return [norm1] + [branch(n) for n in range(N_BRANCH)] + [project, norm2, router]


def _mixmerge_body(x_ref, attc_ref, attl_ref, a_ref, g_ref, p_ref, u_ref, v_ref,
                   ap_ref, gp_ref, pp_ref, an_ref, gn_ref, pn_ref,
                   cw_ref, cb_ref, clg_ref, clb_ref, pw_ref, ps_ref, slg_ref, slb_ref, sw_ref, sbt_ref,
                   a2_ref, g2c_ref, p2_ref, u2_ref, v2_ref, an2_ref, gn2_ref, pn2_ref,
                   mod_ref, g1_ref, g2_ref, wg_ref, wb_ref, wo_ref, wrh_ref, wrl_ref,
                   xm_ref, h2_ref, afft_ref,
                   ypad, zpad, yshift, acc_scr, ug_scr, vn_scr, h_scr, merged_scr, h2lo_scr, br_cur, br_next):
    i = pl.program_id(0)
    tm = 2 * CHUNK
    seg = _segment_of_row(i * tm)
    is_ctx = i < N_CTX // tm
    mix = (cw_ref, cb_ref, clg_ref, clb_ref, pw_ref, ps_ref, slg_ref, slb_ref, sw_ref, sbt_ref)
    mix_work = (ypad, zpad, yshift, acc_scr, ug_scr, vn_scr)
    merge_refs = (x_ref, mod_ref, g1_ref, g2_ref, wg_ref, wb_ref, wo_ref, wrh_ref, wrl_ref,
                  xm_ref, h2_ref, afft_ref)
    merge_work = (h_scr, merged_scr, h2lo_scr)
    first, second = slice(0, CHUNK), slice(CHUNK, 2 * CHUNK)
    mid_prev, mid_next = slice(CHUNK - HALO, CHUNK), slice(CHUNK, CHUNK + HALO)
    tail = slice(2 * CHUNK - HALO, 2 * CHUNK)

    def tile_rows(rows):
        return {"a": lambda: a_ref[rows, :], "g": lambda: g_ref[rows, :], "p": lambda: p_ref[rows, :],
                "u": lambda: u_ref[rows, :], "v": lambda: v_ref[rows, :]}

    def halo(refs, rows=slice(None)):
        return lambda: tuple(r[rows, :] for r in refs)

    tile_agp = (a_ref, g_ref, p_ref)
    src0 = dict(tile_rows(first), prev=halo((ap_ref, gp_ref, pp_ref)), nxt=halo(tile_agp, mid_next))
    src1 = dict(tile_rows(second), prev=halo(tile_agp, mid_prev), nxt=halo((an_ref, gn_ref, pn_ref)))
    src2 = {"a": lambda: a2_ref[...], "g": lambda: g2c_ref[...], "p": lambda: p2_ref[...],
            "u": lambda: u2_ref[...], "v": lambda: v2_ref[...],
            "prev": halo(tile_agp, tail), "nxt": halo((an2_ref, gn2_ref, pn2_ref))}

    @pl.when(i == 0)
    def _():
        for step in _mixers_steps(0, src0, mix, mix_work, br_next):
            step()

    def attention(rows):
        return lambda: jnp.where(is_ctx, attc_ref[rows, :], attl_ref[rows, :])

    _interleave(_merge_steps(first, seg, attention(first), br_next, merge_refs, merge_work),
                _mixers_steps(2 * i + 1, src1, mix, mix_work, br_cur))
    _interleave(_merge_steps(second, seg, attention(second), br_cur, merge_refs, merge_work),
                _mixers_steps(2 * i + 2, src2, mix, mix_work, br_next))


def _mixmerge(x, att_ctx, att_lat, proj, mod, norm1_g, norm2_g, conv_w, conv_b, conv_ln_g, conv_ln_b,
              pool_w_bf, pool_scale, sgu_ln_g, sgu_ln_b, sgu_w_bf, sgu_bt, w_gate_half, w_branch_bf, w_out_half,
              w_router_hi, w_router_lo, l):
    tm = 512
    n_ctx_tiles = N_CTX // tm
    n_lat_tiles = N_LAT // tm
    halos_per_tile = tm // HALO
    n_halo = N_TOK // HALO

    def rows(w):
        return pl.BlockSpec((tm, w), lambda i: (i, 0))

    def main(col):
        return pl.BlockSpec((tm, COL_BLK), lambda i: (i, col))

    def prev(col):
        return pl.BlockSpec((HALO, COL_BLK), lambda i: (jnp.maximum(i * halos_per_tile - 1, 0), col))

    def nxt(col):
        return pl.BlockSpec((HALO, COL_BLK), lambda i: (jnp.minimum((i + 1) * halos_per_tile, n_halo - 1), col))

    def ahead(col):
        return pl.BlockSpec((CHUNK, COL_BLK), lambda i: (jnp.minimum(2 * i + 2, N_TOK // CHUNK - 1), col))

    def ahead_nxt(col):
        return pl.BlockSpec(
            (HALO, COL_BLK),
            lambda i: (jnp.minimum((i + 1) * halos_per_tile + CHUNK // HALO, n_halo - 1), col))

    def vec(n):
        return pl.BlockSpec((None, 1, n), lambda i: (l, 0, 0))

    def const(shape, index_map):
        return pl.BlockSpec(shape, index_map, pipeline_mode=pl.Buffered(1))

    return pl.pallas_call(
        _mixmerge_body,
        grid=(N_TOK // tm,),
        in_specs=[
            rows(D_MODEL),
            pl.BlockSpec((tm, W_ATT), lambda i: (jnp.minimum(i, n_ctx_tiles - 1), 0)),
            pl.BlockSpec((tm, W_ATT), lambda i: (jnp.clip(i - n_ctx_tiles, 0, n_lat_tiles - 1), 0)),
            main(3), main(4), main(5), main(6), main(7),
            prev(3), prev(4), prev(5), nxt(3), nxt(4), nxt(5),
            pl.BlockSpec((None, CONV_K, W_CONV), lambda i: (l, 0, 0)),
            vec(W_CONV), vec(W_CONV), vec(W_CONV),
            pl.BlockSpec((None, len(POOL_WINDOWS), POOL_GROUP, POOL_GROUP), lambda i: (l, 0, 0, 0)),
            vec(W_POOL), vec(W_SGU), vec(W_SGU),
            pl.BlockSpec((None, SGU_GROUPS, SGU_CHUNK, SGU_CHUNK), lambda i: (l, 0, 0, 0)),
            pl.BlockSpec((None, SGU_CHUNK, SGU_GROUPS), lambda i: (l, 0, 0)),
            ahead(3), ahead(4), ahead(5), ahead(6), ahead(7), ahead_nxt(3), ahead_nxt(4), ahead_nxt(5),
            pl.BlockSpec((None, 8, 6 * D_MODEL), lambda i: (l, 0, 0)),
            vec(D_MODEL), vec(D_MODEL),
            const((None, D_MODEL, N_BRANCH * D_MODEL), lambda i: (l, 0, 0)),
            const((None, N_BRANCH, BRANCH_W, D_MODEL), lambda i: (l, 0, 0, 0)),
            const((None, D_MODEL, D_MODEL), lambda i: (l, 0, 0)),
            const((None, D_MODEL, LANES), lambda i: (l, 0, 0)),
            const((None, D_MODEL, LANES), lambda i: (l, 0, 0)),
        ],
        out_specs=[rows(D_MODEL), rows(D_MODEL), pl.BlockSpec((N_EXPERTS, tm), lambda i: (0, i))],
        out_shape=[
            jax.ShapeDtypeStruct((N_TOK, D_MODEL), F32),
            jax.ShapeDtypeStruct((N_TOK, D_MODEL), BF16),
            jax.ShapeDtypeStruct((N_EXPERTS, N_TOK), F32),
        ],
        scratch_shapes=[
            pltpu.VMEM((CHUNK + 2 * HALO, W_CONV), F32),
            pltpu.VMEM((CHUNK + 2 * HALO, W_POOL), F32),
            pltpu.VMEM((SUBLANES, CHUNK + 2 * HALO - SUBLANES, W_CONV), F32),
            pltpu.VMEM((CHUNK, W_CONV), F32),
            pltpu.VMEM((CHUNK, W_SGU), F32),
            pltpu.VMEM((CHUNK, W_SGU), BF16),
            pltpu.VMEM((CHUNK, D_MODEL), BF16),
            pltpu.VMEM((CHUNK, D_MODEL), F32),
            pltpu.VMEM((CHUNK, D_MODEL), BF16),
            pltpu.VMEM((3, CHUNK, BRANCH_W), BF16),
            pltpu.VMEM((3, CHUNK, BRANCH_W), BF16),
        ],
        compiler_params=_cparams(1, 56),
        name="mixmerge",
    )(x, att_ctx, att_lat, *([proj] * 11), conv_w, conv_b, conv_ln_g, conv_ln_b, pool_w_bf, pool_scale,
      sgu_ln_g, sgu_ln_b, sgu_w_bf, sgu_bt, *([proj] * 8), mod, norm1_g, norm2_g, w_gate_half, w_branch_bf,
      w_out_half, w_router_hi, w_router_lo)


def _route_body(afft_ref, tri_ref, pos_ref, gate_ref, *, n_seq, seq_len, cap):
    a = jnp.concatenate([afft_ref[:, s * seq_len:(s + 1) * seq_len] for s in range(n_seq)], axis=0)
    n_rows = n_seq * N_EXPERTS
    capf = float(cap)

    def count(mask):
        return jnp.sum(jnp.where(mask, 1.0, 0.0), axis=1, keepdims=True)

    def as_float(bits):
        return lax.bitcast_convert_type(bits, F32)

    thr = jnp.zeros((n_rows, 1), jnp.int32)
    for bit in range(30, -1, -1):
        cand = thr | (1 << bit)
        thr = jnp.where(count(a >= as_float(cand)) >= capf, cand, thr)
    above = a >= as_float(thr + 1)
    tie = jnp.logical_and(a >= as_float(thr), jnp.logical_not(above))
    need = capf - count(above)

    idx = lax.broadcasted_iota(jnp.int32, (n_rows, seq_len), 1)
    last = jnp.zeros((n_rows, 1), jnp.int32)
    for bit in range(seq_len.bit_length() - 2, -1, -1):
        cand = last | (1 << bit)
        before = jnp.sum(jnp.where(tie, jnp.where(idx < cand, 1.0, 0.0), 0.0), axis=1, keepdims=True)
        last = jnp.where(before < need, cand, last)
    keep = jnp.where(above, 1.0, jnp.where(tie, jnp.where(idx <= last, 1.0, 0.0), 0.0))

    blk = tri_ref.shape[0]
    offset = jnp.zeros((n_rows, 1), F32)
    pos = []
    for c in range(seq_len // blk):
        kb = keep[:, c * blk:(c + 1) * blk]
        pos.append(_dot(kb.astype(BF16), tri_ref[...]) + offset)
        offset = offset + jnp.sum(kb, axis=1, keepdims=True)
    pos = jnp.concatenate(pos, axis=1)
    pos_ref[...] = jnp.where(keep > 0.0, pos, -1.0)
    gate_ref[...] = a


def _route(aff_t, tri, n_seq, seq_len, cap, col0):
    n_rows = n_seq * N_EXPERTS
    width = n_seq * seq_len
    shape = jax.ShapeDtypeStruct((n_rows, seq_len), F32)
    spec = pl.BlockSpec((n_rows, seq_len), lambda i: (0, 0))
    return pl.pallas_call(
        functools.partial(_route_body, n_seq=n_seq, seq_len=seq_len, cap=cap),
        grid=(1,),
        in_specs=[
            pl.BlockSpec((N_EXPERTS, width), lambda i: (0, col0 // width)),
            pl.BlockSpec(tri.shape, lambda i: (0, 0)),
        ],
        out_specs=[spec, spec],
        out_shape=[shape, shape],
        compiler_params=_cparams(1, 32),
        name="route_ctx" if cap == CAP_CTX else "route_lat",
    )(aff_t, tri)


def _slot_onehot(pos_ref, cap, values_ref=None, seq=0):
    n_tok = pos_ref.shape[1]
    slot = lax.broadcasted_iota(jnp.int32, (cap, n_tok), 0).astype(F32)
    rows = []
    for e in range(seq * N_EXPERTS, (seq + 1) * N_EXPERTS):
        hit = pos_ref[e:e + 1, :] == slot
        val = 1.0 if values_ref is None else values_ref[e:e + 1, :]
        rows.append(jnp.where(hit, val, 0.0).astype(BF16))
    return jnp.concatenate(rows, axis=0)


def _gather_ctx_body(pos_ref, h_ref, o_ref):
    for s in range(MOE_SEQS):
        onehot = _slot_onehot(pos_ref, CAP_CTX, seq=s)
        picked = _dot(onehot, h_ref[s * SEQ:(s + 1) * SEQ, :]).astype(BF16)
        o_ref[:, s * CAP_CTX:(s + 1) * CAP_CTX, :] = picked.reshape(N_EXPERTS, CAP_CTX, D_MODEL)


def _gather_ctx(pos, h2):
    return pl.pallas_call(
        _gather_ctx_body,
        grid=(BATCH // MOE_SEQS,),
        in_specs=[
            pl.BlockSpec((MOE_SEQS * N_EXPERTS, SEQ), lambda s: (s, 0)),
            pl.BlockSpec((MOE_SEQS * SEQ, D_MODEL), lambda s: (s, 0)),
        ],
        out_specs=pl.BlockSpec((N_EXPERTS, MOE_SEQS * CAP_CTX, D_MODEL), lambda s: (0, s, 0)),
        out_shape=jax.ShapeDtypeStruct((N_EXPERTS, BATCH * CAP_CTX, D_MODEL), BF16),
        compiler_params=_cparams(1, 32),
        name="gather_ctx",
    )(pos, h2)


def _gather_lat_body(pos_ref, h_ref, o_ref):
    e0 = pl.program_id(1) * LAT_EXPERTS
    slot = lax.broadcasted_iota(jnp.int32, (CAP_LAT, DEC_SEQ), 0).astype(F32)
    onehot = jnp.concatenate(
        [jnp.where(pos_ref[pl.ds(e0 + n, 1), :] == slot, 1.0, 0.0).astype(BF16) for n in range(LAT_EXPERTS)],
        axis=0)
    picked = _dot(onehot, h_ref[...]).astype(BF16)
    o_ref[...] = picked.reshape(LAT_EXPERTS, CAP_LAT, D_MODEL)


def _gather_lat(pos, h2):
    blk0 = N_CTX // DEC_SEQ
    return pl.pallas_call(
        _gather_lat_body,
        grid=(DEC_BATCH, N_EXPERTS // LAT_EXPERTS),
        in_specs=[
            pl.BlockSpec((N_EXPERTS, DEC_SEQ), lambda b, e: (b, 0)),
            pl.BlockSpec((DEC_SEQ, D_MODEL), lambda b, e: (blk0 + b, 0)),
        ],
        out_specs=pl.BlockSpec((LAT_EXPERTS, CAP_LAT, D_MODEL), lambda b, e: (e, b, 0)),
        out_shape=jax.ShapeDtypeStruct((N_EXPERTS, DEC_BATCH * CAP_LAT, D_MODEL), BF16),
        compiler_params=_cparams(2, 40),
        name="gather_lat",
    )(pos, h2)


def _ffn_body(xc_ref, xl_ref, w1_ref, w3_ref, w2_ref, yc_ref, yl_ref):
    w1 = w1_ref[...].astype(BF16)
    w3 = w3_ref[...].astype(BF16)
    w2 = w2_ref[...].astype(BF16)
    rows = 512
    for x_ref, y_ref in ((xc_ref, yc_ref), (xl_ref, yl_ref)):
        for r0 in range(0, x_ref.shape[0], rows):
            x = x_ref[r0:r0 + rows, :]
            a = _dot(x, w1)
            act = (a * _sigmoid(a) * _dot(x, w3)).astype(BF16)
            y_ref[r0:r0 + rows, :] = _dot(act, w2).astype(BF16)


def _ffn(xg_ctx, xg_lat, w1, w3, w2, l):
    n_c, n_l = xg_ctx.shape[1], xg_lat.shape[1]

    def wspec():
        return pl.BlockSpec((None, None, D_MODEL, D_MODEL), lambda e: (l, e, 0, 0))

    def xspec(n):
        return pl.BlockSpec((None, n, D_MODEL), lambda e: (e, 0, 0))

    return pl.pallas_call(
        _ffn_body,
        grid=(N_EXPERTS,),
        in_specs=[xspec(n_c), xspec(n_l), wspec(), wspec(), wspec()],
        out_specs=[xspec(n_c), xspec(n_l)],
        out_shape=[jax.ShapeDtypeStruct(xg_ctx.shape, BF16), jax.ShapeDtypeStruct(xg_lat.shape, BF16)],
        compiler_params=_cparams(1, 56),
        name="ffn",
    )(xg_ctx, xg_lat, w1, w3, w2)


def _combine_body(pos_ref, gate_ref, y_ref, x_ref, mod_ref, *rest, cap, seg0, n_seq):
    o_ref = rest[-1]
    seg = seg0 + pl.program_id(0) if seg0 else 0
    n_tok = pos_ref.shape[1]
    for s in range(n_seq):
        rows = slice(s * n_tok, (s + 1) * n_tok)
        weights = _slot_onehot(pos_ref, cap, gate_ref, seq=s)
        y = y_ref[:, s * cap:(s + 1) * cap, :].reshape(N_EXPERTS * cap, D_MODEL)
        out = x_ref[rows, :] + _mod_row(mod_ref, seg, 5) * _dot_tn(weights, y)
        if len(rest) == 2:
            out = out * lax.rsqrt(jnp.mean(out * out, axis=-1, keepdims=True) + EPS) * rest[0][...]
        o_ref[rows, :] = out


def _combine_ctx(pos, gate, y, x_mid, mod, l, final_g=None):
    last = final_g is not None
    return pl.pallas_call(
        functools.partial(_combine_body, cap=CAP_CTX, seg0=0, n_seq=MOE_SEQS),
        grid=(BATCH // MOE_SEQS,),
        in_specs=[
            pl.BlockSpec((MOE_SEQS * N_EXPERTS, SEQ), lambda s: (s, 0)),
            pl.BlockSpec((MOE_SEQS * N_EXPERTS, SEQ), lambda s: (s, 0)),
            pl.BlockSpec((N_EXPERTS, MOE_SEQS * CAP_CTX, D_MODEL), lambda s: (0, s, 0)),
            pl.BlockSpec((MOE_SEQS * SEQ, D_MODEL), lambda s: (s, 0)),
            pl.BlockSpec((None, 8, 6 * D_MODEL), lambda s: (l, 0, 0)),
        ] + ([pl.BlockSpec((1, D_MODEL), lambda s: (0, 0))] if last else []),
        out_specs=pl.BlockSpec((MOE_SEQS * SEQ, D_MODEL), lambda s: (s, 0)),
        out_shape=jax.ShapeDtypeStruct((N_CTX if last else N_TOK, D_MODEL), F32),
        input_output_aliases={} if last else {3: 0},
        compiler_params=_cparams(1, 32),
        name="combine_ctx",
    )(pos, gate, y, x_mid, mod, *([final_g] if last else []))


def _combine_lat(pos, gate, y, x_mid, mod, l, final_g=None):
    last = final_g is not None
    tiles = DEC_SEQ // LAT_TOKENS
    blk0 = N_CTX // LAT_TOKENS
    return pl.pallas_call(
        functools.partial(_combine_body, cap=CAP_LAT, seg0=1, n_seq=1),
        grid=(DEC_BATCH, tiles),
        in_specs=[
            pl.BlockSpec((N_EXPERTS, LAT_TOKENS), lambda b, t: (b, t)),
            pl.BlockSpec((N_EXPERTS, LAT_TOKENS), lambda b, t: (b, t)),
            pl.BlockSpec((N_EXPERTS, CAP_LAT, D_MODEL), lambda b, t: (0, b, 0)),
            pl.BlockSpec((LAT_TOKENS, D_MODEL), lambda b, t: (blk0 + b * tiles + t, 0)),
            pl.BlockSpec((None, 8, 6 * D_MODEL), lambda b, t: (l, 0, 0)),
        ] + ([pl.BlockSpec((1, D_MODEL), lambda b, t: (0, 0))] if last else []),
        out_specs=pl.BlockSpec((LAT_TOKENS, D_MODEL), lambda b, t: ((0 if last else blk0) + b * tiles + t, 0)),
        out_shape=jax.ShapeDtypeStruct((N_LAT if last else N_TOK, D_MODEL), F32),
        input_output_aliases={} if last else {3: 0},
        compiler_params=_cparams(2, 48),
        name="combine_lat",
    )(pos, gate, y, x_mid, mod, *([final_g] if last else []))


def kernel(x_prompt, x_sample, cache_k, cache_v, c, c_ctx, w_ada, b_ada, norm1_g, w_in, rpb, conv_w, conv_b,
           conv_ln_g, conv_ln_b, pool_w, pool_scale, sgu_ln_g, sgu_ln_b, sgu_w, sgu_b, w_branch, w_out,
           norm2_g, w_router, w1, w3, w2, final_g):
    def vecs(a):
        return a.reshape(DEPTH, 1, a.shape[-1])

    cond8 = jnp.concatenate([c_ctx[None, :], c, jnp.zeros((8 - 1 - DEC_BATCH, D_MODEL), F32)], axis=0)
    mod = _adaln(cond8, w_ada, b_ada)
    table = _rpb_table(rpb)
    cos_np, sin_np = _rope_tables()
    cos, sin = jnp.asarray(cos_np), jnp.asarray(sin_np)
    tri = jnp.asarray(np.triu(np.ones((CHUNK, CHUNK), np.float32), 1), BF16)

    w_gate_half = (0.5 * w_in[:, :, D_PROJ:]).astype(BF16)
    w_proj_bf = w_in[:, :, :D_PROJ].astype(BF16)
    w_branch_bf = w_branch.astype(BF16)
    w_out_half = (0.5 * w_out).astype(BF16)
    pool_w_bf = pool_w.astype(BF16)
    sgu_w_bf = sgu_w.astype(BF16)
    sgu_bt = sgu_b.transpose(0, 2, 1)
    w_router_pad = jnp.pad(w_router, ((0, 0), (0, 0), (0, LANES - N_EXPERTS)))
    w_router_hi = w_router_pad.astype(BF16)
    w_router_lo = (w_router_pad - w_router_hi.astype(F32)).astype(BF16)
    norm1_v, norm2_v = vecs(norm1_g), vecs(norm2_g)
    conv_b_v, conv_ln_g_v, conv_ln_b_v = vecs(conv_b), vecs(conv_ln_g), vecs(conv_ln_b)
    pool_scale_v, sgu_ln_g_v, sgu_ln_b_v = vecs(pool_scale), vecs(sgu_ln_g), vecs(sgu_ln_b)

    x = jnp.concatenate([x_prompt.reshape(N_CTX, D_MODEL), x_sample.reshape(N_LAT, D_MODEL)], axis=0)
    projs = []
    for l in range(DEPTH):
        proj = _proj(x, norm1_v, mod, w_proj_bf, l)
        projs.append(proj)
        att_ctx = _attn_ctx(proj)
        att_lat = _attn_lat(proj, cache_k, cache_v, table, cos, sin, l)
        x_mid, h2, aff_t = _mixmerge(x, att_ctx, att_lat, proj, mod, norm1_v, norm2_v, conv_w, conv_b_v,
                                     conv_ln_g_v, conv_ln_b_v, pool_w_bf, pool_scale_v, sgu_ln_g_v, sgu_ln_b_v,
                                     sgu_w_bf, sgu_bt, w_gate_half, w_branch_bf, w_out_half, w_router_hi,
                                     w_router_lo, l)
        pos_c, gate_c = _route(aff_t, tri, BATCH, SEQ, CAP_CTX, 0)
        pos_l, gate_l = _route(aff_t, tri, DEC_BATCH, DEC_SEQ, CAP_LAT, N_CTX)
        y_c, y_l = _ffn(_gather_ctx(pos_c, h2), _gather_lat(pos_l, h2), w1, w3, w2, l)
        if l < DEPTH - 1:
            x = _combine_ctx(pos_c, gate_c, y_c, x_mid, mod, l)
            x = _combine_lat(pos_l, gate_l, y_l, x, mod, l)
        else:
            y_prompt = _combine_ctx(pos_c, gate_c, y_c, x_mid, mod, l, final_g[None, :])
            y_sample = _combine_lat(pos_l, gate_l, y_l, x_mid, mod, l, final_g[None, :])

    y_prompt = y_prompt.reshape(BATCH, SEQ, D_MODEL)
    y_sample = y_sample.reshape(DEC_BATCH, DEC_SEQ, D_MODEL)
    new_k, new_v = _kv_out(projs)
    return (y_prompt, y_sample, new_k, new_v)
```

```python
import functools

import numpy as np
import jax
import jax.numpy as jnp
from jax import lax
from jax.experimental import pallas as pl
from jax.experimental.pallas import tpu as pltpu

F32 = jnp.float32
BF16 = jnp.bfloat16

D_MODEL = 1024
BATCH = 32
SEQ = 256
DEPTH = 4
DEC_BATCH = 2
DEC_SEQ = 2048
PAST_LEN = 512
GRID_W = 64
GRID_ROWS = DEC_SEQ // GRID_W
N_HEADS = 8
HEAD_DIM = 64
W_ATT = N_HEADS * HEAD_DIM
NA_ROWS = 8
NA_COLS = 16
ROPE_BASE = 10000.0
W_CONV = 512
CONV_K = 31
W_POOL = 512
POOL_WINDOWS = (2, 4, 8, 16)
POOL_GROUP = 128
W_SGU = 512
SGU_GROUPS = 4
SGU_CHUNK = 128
N_BRANCH = 4
BRANCH_W = 512
N_EXPERTS = 16
EC_CAPACITY = 2
EPS = 1e-6
NEG_INF = -1e30

N_CTX = BATCH * SEQ
N_LAT = DEC_BATCH * DEC_SEQ
N_TOK = N_CTX + N_LAT
D_PROJ = 4096
COL_BLK = 512
CAP_CTX = EC_CAPACITY * SEQ // N_EXPERTS
CAP_LAT = EC_CAPACITY * DEC_SEQ // N_EXPERTS
N_ROW_OFF = 2 * NA_ROWS
Q_ROWS = 4
KEY_ROWS = 12
LAT_EXPERTS = 4
LAT_TOKENS = 512
ATT_SEQS = 4
MOE_SEQS = 4
KV_SEQS = 4
HEAD_GROUP = 4
HALO = 16
CHUNK = 256
LANES = 128
SUBLANES = 8
VMEM_LIMIT_CAP = 56 * 1024 * 1024

HIGHEST = lax.Precision.HIGHEST


def _cparams(n_grid, vmem_mb):
    return pltpu.CompilerParams(
        dimension_semantics=("arbitrary",) * n_grid,
        vmem_limit_bytes=min(vmem_mb * 1024 * 1024, VMEM_LIMIT_CAP))


def _dot(a, b):
    return jnp.dot(a, b, preferred_element_type=F32)


def _dot_tn(a, b):
    return lax.dot_general(a, b, (((0,), (0,)), ((), ())), preferred_element_type=F32)


def _sigmoid(x):
    return 0.5 * jnp.tanh(0.5 * x) + 0.5


def _segment_of_row(row0):
    return jnp.where(row0 < N_CTX, 0, 1 + (row0 - N_CTX) // DEC_SEQ)


def _mod_row(mod_ref, seg, k):
    return mod_ref[pl.ds(seg, 1), k * D_MODEL:(k + 1) * D_MODEL]


def _norm_modulate(x, g, shift, scale):
    y = x * lax.rsqrt(jnp.mean(x * x, axis=-1, keepdims=True) + EPS) * g
    return y * (1.0 + scale) + shift


def _layernorm(x, g, b):
    mu = jnp.mean(x, axis=-1, keepdims=True)
    xc = x - mu
    var = jnp.mean(xc * xc, axis=-1, keepdims=True)
    return xc * lax.rsqrt(var + EPS) * g + b


def _adaln_body(cond_ref, w_ref, b_ref, o_ref):
    c = cond_ref[...]
    a = (c * _sigmoid(c)).astype(BF16)
    o_ref[...] = _dot(a, w_ref[...].astype(BF16)) + b_ref[...]


def _adaln(cond8, w_ada, b_ada):
    tn = 1536
    n6 = 6 * D_MODEL
    return pl.pallas_call(
        _adaln_body,
        grid=(DEPTH, n6 // tn),
        in_specs=[
            pl.BlockSpec((8, D_MODEL), lambda l, j: (0, 0)),
            pl.BlockSpec((None, D_MODEL, tn), lambda l, j: (l, 0, j)),
            pl.BlockSpec((None, 1, tn), lambda l, j: (l, 0, j)),
        ],
        out_specs=pl.BlockSpec((None, 8, tn), lambda l, j: (l, 0, j)),
        out_shape=jax.ShapeDtypeStruct((DEPTH, 8, n6), F32),
        compiler_params=_cparams(2, 32),
        name="adaln",
    )(cond8, w_ada, b_ada.reshape(DEPTH, 1, n6))


def _rpb_onehots():
    cq = np.arange(GRID_W)[None, :]
    kc = np.arange(GRID_W)[:, None]
    start = np.clip(cq - NA_COLS // 2, 0, GRID_W - NA_COLS)
    valid = (kc >= start) & (kc < start + NA_COLS)
    d_col = np.clip(kc - cq, -(NA_COLS - 1), NA_COLS - 1) + (NA_COLS - 1)
    left = np.zeros((32, GRID_W, 2 * GRID_W), np.float32)
    right = np.zeros((32, GRID_W, 2 * GRID_W), np.float32)
    mask = np.zeros((GRID_W, 2 * GRID_W), np.float32)
    for j in range(2 * NA_COLS - 1):
        hit = (d_col == j) & valid
        left[j, :, :GRID_W] = hit
        right[j, :, GRID_W:] = hit
    mask[:, :GRID_W] = np.where(valid, 0.0, NEG_INF)
    mask[:, GRID_W:] = np.where(valid, 0.0, NEG_INF)
    n = GRID_W * 2 * GRID_W
    return left.reshape(32, n), right.reshape(32, n), mask.reshape(1, n)


def _rpb_body(a_ref, b_ref, left_ref, right_ref, mask_ref, o_ref):
    t = jnp.dot(a_ref[...], left_ref[...], precision=HIGHEST, preferred_element_type=F32)
    t = t + jnp.dot(b_ref[...], right_ref[...], precision=HIGHEST, preferred_element_type=F32)
    o_ref[...] = t + mask_ref[...]


def _rpb_table(rpb):
    left, right, mask = _rpb_onehots()
    n_rows = N_HEADS * N_ROW_OFF
    rpb_p = jnp.pad(rpb, ((0, 0), (0, 0), (0, 0), (0, 1)))
    a = jnp.pad(rpb_p, ((0, 0), (0, 0), (0, 1), (0, 0))).reshape(DEPTH, n_rows, 32)
    b = jnp.pad(rpb_p, ((0, 0), (0, 0), (1, 0), (0, 0))).reshape(DEPTH, n_rows, 32)
    n = left.shape[1]
    out = pl.pallas_call(
        _rpb_body,
        grid=(DEPTH,),
        in_specs=[
            pl.BlockSpec((None, n_rows, 32), lambda l: (l, 0, 0)),
            pl.BlockSpec((None, n_rows, 32), lambda l: (l, 0, 0)),
            pl.BlockSpec((32, n), lambda l: (0, 0)),
            pl.BlockSpec((32, n), lambda l: (0, 0)),
            pl.BlockSpec((1, n), lambda l: (0, 0)),
        ],
        out_specs=pl.BlockSpec((None, n_rows, n), lambda l: (l, 0, 0)),
        out_shape=jax.ShapeDtypeStruct((DEPTH, n_rows, n), F32),
        compiler_params=_cparams(1, 32),
        name="rpb_table",
    )(a, b, jnp.asarray(left), jnp.asarray(right), jnp.asarray(mask))
    return out.reshape(DEPTH, N_HEADS, N_ROW_OFF, GRID_W, 2 * GRID_W)


def _proj_body(x_ref, g_ref, mod_ref, w_ref, o_ref, h_scr, *, tm):
    i = pl.program_id(0)

    @pl.when(pl.program_id(1) == 0)
    def _():
        seg = _segment_of_row(i * tm)
        h = _norm_modulate(x_ref[...], g_ref[...], _mod_row(mod_ref, seg, 0), _mod_row(mod_ref, seg, 1))
        h_scr[...] = h.astype(BF16)

    o_ref[...] = _dot(h_scr[...], w_ref[...]).astype(BF16)


def _proj(x, norm_g, mod, w_proj_bf, l):
    tm, tn = 1024, 4096
    return pl.pallas_call(
        functools.partial(_proj_body, tm=tm),
        grid=(N_TOK // tm, D_PROJ // tn),
        in_specs=[
            pl.BlockSpec((tm, D_MODEL), lambda i, j: (i, 0)),
            pl.BlockSpec((None, 1, D_MODEL), lambda i, j: (l, 0, 0)),
            pl.BlockSpec((None, 8, 6 * D_MODEL), lambda i, j: (l, 0, 0)),
            pl.BlockSpec((None, D_MODEL, tn), lambda i, j: (l, 0, j)),
        ],
        out_specs=pl.BlockSpec((tm, tn), lambda i, j: (i, j)),
        out_shape=jax.ShapeDtypeStruct((N_TOK, D_PROJ), BF16),
        scratch_shapes=[pltpu.VMEM((tm, D_MODEL), BF16)],
        compiler_params=_cparams(2, 56),
        name="proj",
    )(x, norm_g, mod, w_proj_bf)


def _head_rows_in_pair(x_t, h):
    rows = x_t[h * HEAD_DIM:(h + 1) * HEAD_DIM, :]
    zeros = jnp.zeros_like(rows)
    return jnp.concatenate([rows, zeros] if h % 2 == 0 else [zeros, rows], axis=0)


def _pair_lanes(h):
    return slice((h // 2) * LANES, (h // 2 + 1) * LANES)


def _attn_ctx_body(q_ref, k_ref, v_ref, o_ref):
    heads = range(N_HEADS)
    for s in range(ATT_SEQS):
        rows = slice(s * SEQ, (s + 1) * SEQ)
        k = k_ref[rows, :]
        q_t = (q_ref[rows, :].astype(F32) * (HEAD_DIM ** -0.5)).T.astype(BF16)
        v_t = v_ref[rows, :].astype(F32).T.astype(BF16)
        s_t = [_dot(k[:, _pair_lanes(h)], _head_rows_in_pair(q_t, h)) for h in heads]
        e_t = [jnp.exp(a - jnp.max(a, axis=0, keepdims=True)) for a in s_t]
        denom = [jnp.sum(e, axis=0, keepdims=True) for e in e_t]
        o_t = [_dot(v_t[h * HEAD_DIM:(h + 1) * HEAD_DIM, :], e_t[h].astype(BF16)) / denom[h] for h in heads]
        o_ref[rows, :] = jnp.concatenate(o_t, axis=0).T.astype(BF16)


def _attn_ctx(proj):
    blk = ATT_SEQS * SEQ
    return pl.pallas_call(
        _attn_ctx_body,
        grid=(BATCH // ATT_SEQS,),
        in_specs=[
            pl.BlockSpec((blk, COL_BLK), lambda b: (b, 0)),
            pl.BlockSpec((blk, COL_BLK), lambda b: (b, 1)),
            pl.BlockSpec((blk, COL_BLK), lambda b: (b, 2)),
        ],
        out_specs=pl.BlockSpec((blk, W_ATT), lambda b: (b, 0)),
        out_shape=jax.ShapeDtypeStruct((N_CTX, W_ATT), BF16),
        compiler_params=_cparams(1, 32),
        name="attn_ctx",
    )(proj, proj, proj)


def _kv_out_body(*refs):
    kv_refs, (ck_ref, cv_ref) = refs[:2 * DEPTH], refs[2 * DEPTH:]
    l = pl.program_id(0)
    for j in range(DEPTH):
        @pl.when(l == j)
        def _(j=j):
            for s in range(KV_SEQS):
                rows = slice(s * SEQ, (s + 1) * SEQ)
                k = kv_refs[2 * j][rows, :]
                v = kv_refs[2 * j + 1][rows, :]
                for h in range(N_HEADS):
                    sl = slice(h * HEAD_DIM, (h + 1) * HEAD_DIM)
                    ck_ref[s, h] = k[:, sl].astype(F32)
                    cv_ref[s, h] = v[:, sl].astype(F32)


def _kv_out(projs):
    steps = BATCH // KV_SEQS

    def src(j, col):
        return pl.BlockSpec((KV_SEQS * SEQ, COL_BLK),
                            lambda l, b: (jnp.where(l == j, b, jnp.where(l < j, 0, steps - 1)), col))

    cache_spec = pl.BlockSpec((KV_SEQS, None, N_HEADS, SEQ, HEAD_DIM), lambda l, b: (b, l, 0, 0, 0))
    cache_shape = jax.ShapeDtypeStruct((BATCH, DEPTH, N_HEADS, SEQ, HEAD_DIM), F32)
    in_specs, args = [], []
    for j in range(DEPTH):
        in_specs += [src(j, 1), src(j, 2)]
        args += [projs[j], projs[j]]
    return pl.pallas_call(
        _kv_out_body,
        grid=(DEPTH, steps),
        in_specs=in_specs,
        out_specs=[cache_spec, cache_spec],
        out_shape=[cache_shape, cache_shape],
        compiler_params=_cparams(2, 48),
        name="kv_out",
    )(*args)


def _rope_tables():
    t = np.arange(DEC_SEQ)
    half = HEAD_DIM // 2
    nf = half // 2
    inv = (1.0 / (ROPE_BASE ** (np.arange(nf) / nf))).astype(np.float32)
    cos = np.zeros((DEC_SEQ, HEAD_DIM), np.float32)
    sin = np.zeros((DEC_SEQ, HEAD_DIM), np.float32)
    for blk, pos in enumerate((t // GRID_W, t % GRID_W)):
        ang = (pos[:, None].astype(np.float32) * inv[None, :]).astype(np.float32)
        c, s = np.cos(ang), np.sin(ang)
        cos[:, blk * half:(blk + 1) * half] = np.concatenate([c, c], axis=1)
        sin[:, blk * half:(blk + 1) * half] = np.concatenate([-s, s], axis=1)
    return np.tile(cos, (1, 2)), np.tile(sin, (1, 2))


def _rope(x, cos, sin):
    cos = jnp.concatenate([cos] * (W_ATT // LANES), axis=1)
    sin = jnp.concatenate([sin] * (W_ATT // LANES), axis=1)
    lane = lax.broadcasted_iota(jnp.int32, x.shape, 1)
    nf = HEAD_DIM // 4
    partner = jnp.where(lane % (2 * nf) < nf, pltpu.roll(x, W_ATT - nf, 1), pltpu.roll(x, nf, 1))
    return x * cos + partner * sin


def _attn_lat_body(q_ref, k_ref, v_ref, ck_ref, cv_ref, tab_ref, cos_ref, sin_ref, o_ref,
                   krot_scr, vt_scr, ck_scr, cvt_scr):
    g = pl.program_id(1)
    blk = Q_ROWS * GRID_W
    n_blk = DEC_SEQ // blk

    @pl.when(g == 0)
    def _():
        def chunk(c, carry):
            rows = pl.ds(pl.multiple_of(c * blk, blk), blk)
            kr = _rope(k_ref[rows, :].astype(F32), cos_ref[rows, :], sin_ref[rows, :])
            krot_scr[rows, :] = kr.astype(BF16)
            vt_scr[c] = v_ref[rows, :].astype(F32).T.astype(BF16)
            return carry
        lax.fori_loop(0, n_blk, chunk, 0)
        for h in range(0, N_HEADS, 2):
            ck_scr[h] = ck_ref[h].astype(BF16)
            ck_scr[h + 1] = ck_ref[h + 1].astype(BF16)
            pair_t = jnp.concatenate([cv_ref[h], cv_ref[h + 1]], axis=1).T
            cvt_scr[h] = pair_t[:HEAD_DIM, :].astype(BF16)
            cvt_scr[h + 1] = pair_t[HEAD_DIM:, :].astype(BF16)

    kh = min(NA_ROWS, GRID_ROWS)
    scale = HEAD_DIM ** -0.5
    r0 = g * Q_ROWS
    blk0 = jnp.clip(g - 1, 0, n_blk - KEY_ROWS // Q_ROWS)
    u = blk0 * Q_ROWS

    qrows = pl.ds(pl.multiple_of(g * blk, blk), blk)
    q = q_ref[...].astype(F32)
    q_rot_t = (_rope(q, cos_ref[qrows, :], sin_ref[qrows, :]) * scale).T.astype(BF16)
    q_plain_t = (q * scale).T.astype(BF16)
    k_win = krot_scr[pl.ds(pl.multiple_of(blk0 * blk, blk), KEY_ROWS * GRID_W), :]

    lane = lax.broadcasted_iota(jnp.int32, (GRID_W, 2 * GRID_W), 1)
    row_masks = []
    for kr in range(KEY_ROWS):
        pair_masks = []
        for ip in range(Q_ROWS // 2):
            m = []
            for i in (2 * ip, 2 * ip + 1):
                start = jnp.clip(r0 + i - kh // 2, 0, GRID_ROWS - kh)
                inside = jnp.logical_and(u + kr >= start, u + kr < start + kh)
                m.append(jnp.where(inside, 0.0, NEG_INF))
            pair_masks.append(jnp.where(lane < GRID_W, m[0], m[1]))
        row_masks.append(jnp.concatenate(pair_masks, axis=1))

    def head_rows(h):
        return slice(h * HEAD_DIM, (h + 1) * HEAD_DIM)

    def local_bias(h):
        bias_rows = []
        for kr in range(KEY_ROWS):
            tiles = []
            for ip in range(Q_ROWS // 2):
                d_row = u + kr - (r0 + 2 * ip) + (NA_ROWS - 1)
                tiles.append(tab_ref[h, jnp.clip(d_row, 0, N_ROW_OFF - 1)])
            bias_rows.append(jnp.concatenate(tiles, axis=1) + row_masks[kr])
        return jnp.concatenate(bias_rows, axis=0)

    outs = []
    for h0 in range(0, N_HEADS, HEAD_GROUP):
        heads = range(h0, h0 + HEAD_GROUP)
        s_loc = [_dot(k_win[:, _pair_lanes(h)], _head_rows_in_pair(q_rot_t, h)) + local_bias(h)
                 for h in heads]
        s_ctx = [_dot(ck_scr[h], q_plain_t[head_rows(h), :]) for h in heads]
        m = [jnp.maximum(jnp.max(a, axis=0, keepdims=True), jnp.max(b, axis=0, keepdims=True))
             for a, b in zip(s_loc, s_ctx)]
        e_loc = [jnp.exp(a - mm) for a, mm in zip(s_loc, m)]
        e_ctx = [jnp.exp(b - mm) for b, mm in zip(s_ctx, m)]
        denom = [jnp.sum(a, axis=0, keepdims=True) + jnp.sum(b, axis=0, keepdims=True)
                 for a, b in zip(e_loc, e_ctx)]
        for i, h in enumerate(heads):
            p_loc = e_loc[i].astype(BF16)
            o_t = _dot(cvt_scr[h], e_ctx[i].astype(BF16))
            for c in range(KEY_ROWS // Q_ROWS):
                o_t = o_t + _dot(vt_scr[blk0 + c, head_rows(h), :], p_loc[c * blk:(c + 1) * blk, :])
            outs.append(o_t / denom[i])
    o_ref[...] = jnp.concatenate(outs, axis=0).T.astype(BF16)


def _attn_lat(proj, cache_k, cache_v, table, cos, sin, l):
    blk = Q_ROWS * GRID_W
    steps = GRID_ROWS // Q_ROWS
    q_blk0 = N_CTX // blk
    kv_blk0 = N_CTX // DEC_SEQ
    cache_spec = pl.BlockSpec((None, None, N_HEADS, PAST_LEN, HEAD_DIM), lambda b, g: (b, l, 0, 0, 0))

    def const(shape, index_map):
        return pl.BlockSpec(shape, index_map, pipeline_mode=pl.Buffered(1))

    return pl.pallas_call(
        _attn_lat_body,
        grid=(DEC_BATCH, steps),
        in_specs=[
            pl.BlockSpec((blk, COL_BLK), lambda b, g: (q_blk0 + b * steps + g, 0)),
            pl.BlockSpec((DEC_SEQ, COL_BLK), lambda b, g: (kv_blk0 + b, 1)),
            pl.BlockSpec((DEC_SEQ, COL_BLK), lambda b, g: (kv_blk0 + b, 2)),
            cache_spec, cache_spec,
            const((None, N_HEADS, N_ROW_OFF, GRID_W, 2 * GRID_W), lambda b, g: (l, 0, 0, 0, 0)),
            const((DEC_SEQ, LANES), lambda b, g: (0, 0)),
            const((DEC_SEQ, LANES), lambda b, g: (0, 0)),
        ],
        out_specs=pl.BlockSpec((blk, W_ATT), lambda b, g: (b * steps + g, 0)),
        out_shape=jax.ShapeDtypeStruct((N_LAT, W_ATT), BF16),
        scratch_shapes=[
            pltpu.VMEM((DEC_SEQ, W_ATT), BF16),
            pltpu.VMEM((DEC_SEQ // blk, W_ATT, blk), BF16),
            pltpu.VMEM((N_HEADS, PAST_LEN, HEAD_DIM), BF16),
            pltpu.VMEM((N_HEADS, HEAD_DIM, PAST_LEN), BF16),
        ],
        compiler_params=_cparams(2, 48),
        name="attn_lat",
    )(proj, proj, proj, cache_k, cache_v, table, cos, sin)


def _interleave(major, minor):
    per = -(-len(minor) // len(major))
    for n, step in enumerate(major):
        step()
        for other in minor[n * per:(n + 1) * per]:
            other()


def _mixers_steps(chunk, src, mix, work, dst):
    cw_ref, cb_ref, clg_ref, clb_ref, pw_ref, ps_ref, slg_ref, slb_ref, sw_ref, sbt_ref = mix
    ypad, zpad, yshift, acc_scr, ug_scr, vn_scr = work
    n_ctx_chunks = N_CTX // CHUNK
    per_seq = DEC_SEQ // CHUNK
    j = (chunk - n_ctx_chunks) % per_seq
    is_lat = chunk >= n_ctx_chunks
    left_ok = jnp.logical_and(is_lat, j > 0)
    right_ok = jnp.logical_and(is_lat, j < per_seq - 1)
    seq_len = jnp.where(is_lat, DEC_SEQ, SEQ)
    chunk_off = jnp.where(is_lat, j * CHUNK, 0)

    def glu(a_, g_):
        return a_.astype(F32) * _sigmoid(g_.astype(F32))

    def fill():
        prev, nxt = src["prev"](), src["nxt"]()
        ypad[0:HALO, :] = jnp.where(left_ok, glu(prev[0], prev[1]), 0.0)
        ypad[HALO:HALO + CHUNK, :] = glu(src["a"](), src["g"]())
        ypad[HALO + CHUNK:, :] = jnp.where(right_ok, glu(nxt[0], nxt[1]), 0.0)
        zpad[0:HALO, :] = jnp.where(left_ok, prev[2].astype(F32), 0.0)
        zpad[HALO:HALO + CHUNK, :] = src["p"]().astype(F32)
        zpad[HALO + CHUNK:, :] = jnp.where(right_ok, nxt[2].astype(F32), 0.0)

    def shift(s):
        def run():
            yshift[s] = ypad[s:s + yshift.shape[1], :]
        return run

    conv_rows = 64
    half_k = CONV_K // 2

    def conv(rb, cg):
        def run():
            lanes = slice(cg * LANES, (cg + 1) * LANES)
            acc = jnp.zeros((conv_rows, LANES), F32)
            for k in range(CONV_K):
                start = HALO + rb * conv_rows + k - half_k
                s = start % SUBLANES
                acc = acc + cw_ref[k:k + 1, lanes] * yshift[s, start - s:start - s + conv_rows, lanes]
            acc_scr[rb * conv_rows:(rb + 1) * conv_rows, lanes] = acc
        return run

    def conv_finish():
        y = _layernorm(acc_scr[...] + cb_ref[...], clg_ref[...], clb_ref[...])
        dst[0] = (y * _sigmoid(y)).astype(BF16)

    def pool(gi, w):
        def run():
            t_seq = lax.broadcasted_iota(jnp.int32, (CHUNK, 1), 0) + chunk_off
            lanes = slice(gi * POOL_GROUP, (gi + 1) * POOL_GROUP)
            lo, hi = -(w // 2), w - w // 2 - 1
            tot = jnp.zeros((CHUNK, POOL_GROUP), F32)
            for jj in range(lo, hi + 1):
                tot = tot + zpad[HALO + jj:HALO + jj + CHUNK, lanes]
            cnt = w - jnp.maximum(0, -lo - t_seq) - jnp.maximum(0, t_seq + hi - (seq_len - 1))
            d = tot / cnt.astype(F32) - zpad[HALO:HALO + CHUNK, lanes]
            dst[1, :, lanes] = (_dot(d.astype(BF16), pw_ref[gi]) * ps_ref[:, lanes]).astype(BF16)
        return run

    def gating_inputs():
        ug_scr[...] = jax.nn.gelu(src["u"]().astype(F32))
        vn_scr[...] = _layernorm(jax.nn.gelu(src["v"]().astype(F32)), slg_ref[...], slb_ref[...]).astype(BF16)

    def gating(n):
        def run():
            pos = slice(n * SGU_CHUNK, (n + 1) * SGU_CHUNK)
            for grp in range(SGU_GROUPS):
                lanes = slice(grp * LANES, (grp + 1) * LANES)
                mixed = _dot(sw_ref[grp], vn_scr[pos, lanes]) + sbt_ref[:, grp:grp + 1]
                dst[2, pos, lanes] = (ug_scr[pos, lanes] * mixed).astype(BF16)
        return run

    steps = [fill] + [shift(s) for s in range(SUBLANES)]
    steps += [conv(rb, cg) for rb in range(CHUNK // conv_rows) for cg in range(W_CONV // LANES)]
    steps += [conv_finish] + [pool(gi, w) for gi, w in enumerate(POOL_WINDOWS)]
    steps += [gating_inputs] + [gating(n) for n in range(CHUNK // SGU_CHUNK)]
    return steps


def _merge_steps(rows, seg, attention, br_ref, merge_refs, work):
    x_ref, mod_ref, g1_ref, g2_ref, wg_ref, wb_ref, wo_ref, wrh_ref, wrl_ref, xm_ref, h2_ref, afft_ref = merge_refs
    h_scr, merged_scr, h2lo_scr = work

    def norm1():
        h_scr[...] = _norm_modulate(x_ref[rows, :], g1_ref[...], _mod_row(mod_ref, seg, 0),
                                    _mod_row(mod_ref, seg, 1)).astype(BF16)

    def branch(n):
        def run():
            half_z = _dot(h_scr[...], wg_ref[:, n * D_MODEL:(n + 1) * D_MODEL])
            gated = (jnp.tanh(half_z) + 1.0) * _dot(attention() if n == 0 else br_ref[n - 1], wb_ref[n])
            merged_scr[...] = gated if n == 0 else merged_scr[...] + gated
        return run

    def project():
        update = _mod_row(mod_ref, seg, 2) * _dot(merged_scr[...].astype(BF16), wo_ref[...])
        xm_ref[rows, :] = x_ref[rows, :] + update

    def norm2():
        h2 = _norm_modulate(xm_ref[rows, :], g2_ref[...], _mod_row(mod_ref, seg, 3), _mod_row(mod_ref, seg, 4))
        h2_hi = h2.astype(BF16)
        h2_ref[rows, :] = h2_hi
        h2lo_scr[...] = (h2 - h2_hi.astype(F32)).astype(BF16)

    def router():
        h2_hi = h2_ref[rows, :]
        h2_lo = h2lo_scr[...]
        logits = _dot(h2_hi, wrh_ref[...]) + (_dot(h2_lo, wrh_ref[...]) + _dot(h2_hi, wrl_ref[...]))
        lane = lax.broadcasted_iota(jnp.int32, logits.shape, 1)
        logits = jnp.where(lane < N_EXPERTS, logits, NEG_INF)
        e = jnp.exp(logits - jnp.max(logits, axis=-1, keepdims=True))
        aff = e / jnp.sum(e, axis=-1, keepdims=True)
        afft_ref[:, rows] = aff.T[:N_EXPERTS, :]

    return [norm1] + [branch(n) for n in range(N_BRANCH)] + [project, norm2, router]


def _mixmerge_body(x_ref, attc_ref, attl_ref, a_ref, g_ref, p_ref, u_ref, v_ref,
                   ap_ref, gp_ref, pp_ref, an_ref, gn_ref, pn_ref,
                   cw_ref, cb_ref, clg_ref, clb_ref, pw_ref, ps_ref, slg_ref, slb_ref, sw_ref, sbt_ref,
                   a2_ref, g2c_ref, p2_ref, u2_ref, v2_ref, an2_ref, gn2_ref, pn2_ref,
                   mod_ref, g1_ref, g2_ref, wg_ref, wb_ref, wo_ref, wrh_ref, wrl_ref,
                   xm_ref, h2_ref, afft_ref,
                   ypad, zpad, yshift, acc_scr, ug_scr, vn_scr, h_scr, merged_scr, h2lo_scr, br_cur, br_next):
    i = pl.program_id(0)
    tm = 2 * CHUNK
    seg = _segment_of_row(i * tm)
    is_ctx = i < N_CTX // tm
    mix = (cw_ref, cb_ref, clg_ref, clb_ref, pw_ref, ps_ref, slg_ref, slb_ref, sw_ref, sbt_ref)
    mix_work = (ypad, zpad, yshift, acc_scr, ug_scr, vn_scr)
    merge_refs = (x_ref, mod_ref, g1_ref, g2_ref, wg_ref, wb_ref, wo_ref, wrh_ref, wrl_ref,
                  xm_ref, h2_ref, afft_ref)
    merge_work = (h_scr, merged_scr, h2lo_scr)
    first, second = slice(0, CHUNK), slice(CHUNK, 2 * CHUNK)
    mid_prev, mid_next = slice(CHUNK - HALO, CHUNK), slice(CHUNK, CHUNK + HALO)
    tail = slice(2 * CHUNK - HALO, 2 * CHUNK)

    def tile_rows(rows):
        return {"a": lambda: a_ref[rows, :], "g": lambda: g_ref[rows, :], "p": lambda: p_ref[rows, :],
                "u": lambda: u_ref[rows, :], "v": lambda: v_ref[rows, :]}

    def halo(refs, rows=slice(None)):
        return lambda: tuple(r[rows, :] for r in refs)

    tile_agp = (a_ref, g_ref, p_ref)
    src0 = dict(tile_rows(first), prev=halo((ap_ref, gp_ref, pp_ref)), nxt=halo(tile_agp, mid_next))
    src1 = dict(tile_rows(second), prev=halo(tile_agp, mid_prev), nxt=halo((an_ref, gn_ref, pn_ref)))
    src2 = {"a": lambda: a2_ref[...], "g": lambda: g2c_ref[...], "p": lambda: p2_ref[...],
            "u": lambda: u2_ref[...], "v": lambda: v2_ref[...],
            "prev": halo(tile_agp, tail), "nxt": halo((an2_ref, gn2_ref, pn2_ref))}

    @pl.when(i == 0)
    def _():
        for step in _mixers_steps(0, src0, mix, mix_work, br_next):
            step()

    def attention(rows):
        return lambda: jnp.where(is_ctx, attc_ref[rows, :], attl_ref[rows, :])

    _interleave(_merge_steps(first, seg, attention(first), br_next, merge_refs, merge_work),
                _mixers_steps(2 * i + 1, src1, mix, mix_work, br_cur))
    _interleave(_merge_steps(second, seg, attention(second), br_cur, merge_refs, merge_work),
                _mixers_steps(2 * i + 2, src2, mix, mix_work, br_next))


def _mixmerge(x, att_ctx, att_lat, proj, mod, norm1_g, norm2_g, conv_w, conv_b, conv_ln_g, conv_ln_b,
              pool_w_bf, pool_scale, sgu_ln_g, sgu_ln_b, sgu_w_bf, sgu_bt, w_gate_half, w_branch_bf, w_out_half,
              w_router_hi, w_router_lo, l):
    tm = 512
    n_ctx_tiles = N_CTX // tm
    n_lat_tiles = N_LAT // tm
    halos_per_tile = tm // HALO
    n_halo = N_TOK // HALO

    def rows(w):
        return pl.BlockSpec((tm, w), lambda i: (i, 0))

    def main(col):
        return pl.BlockSpec((tm, COL_BLK), lambda i: (i, col))

    def prev(col):
        return pl.BlockSpec((HALO, COL_BLK), lambda i: (jnp.maximum(i * halos_per_tile - 1, 0), col))

    def nxt(col):
        return pl.BlockSpec((HALO, COL_BLK), lambda i: (jnp.minimum((i + 1) * halos_per_tile, n_halo - 1), col))

    def ahead(col):
        return pl.BlockSpec((CHUNK, COL_BLK), lambda i: (jnp.minimum(2 * i + 2, N_TOK // CHUNK - 1), col))

    def ahead_nxt(col):
        return pl.BlockSpec(
            (HALO, COL_BLK),
            lambda i: (jnp.minimum((i + 1) * halos_per_tile + CHUNK // HALO, n_halo - 1), col))

    def vec(n):
        return pl.BlockSpec((None, 1, n), lambda i: (l, 0, 0))

    def const(shape, index_map):
        return pl.BlockSpec(shape, index_map, pipeline_mode=pl.Buffered(1))

    return pl.pallas_call(
        _mixmerge_body,
        grid=(N_TOK // tm,),
        in_specs=[
            rows(D_MODEL),
            pl.BlockSpec((tm, W_ATT), lambda i: (jnp.minimum(i, n_ctx_tiles - 1), 0)),
            pl.BlockSpec((tm, W_ATT), lambda i: (jnp.clip(i - n_ctx_tiles, 0, n_lat_tiles - 1), 0)),
            main(3), main(4), main(5), main(6), main(7),
            prev(3), prev(4), prev(5), nxt(3), nxt(4), nxt(5),
            pl.BlockSpec((None, CONV_K, W_CONV), lambda i: (l, 0, 0)),
            vec(W_CONV), vec(W_CONV), vec(W_CONV),
            pl.BlockSpec((None, len(POOL_WINDOWS), POOL_GROUP, POOL_GROUP), lambda i: (l, 0, 0, 0)),
            vec(W_POOL), vec(W_SGU), vec(W_SGU),
            pl.BlockSpec((None, SGU_GROUPS, SGU_CHUNK, SGU_CHUNK), lambda i: (l, 0, 0, 0)),
            pl.BlockSpec((None, SGU_CHUNK, SGU_GROUPS), lambda i: (l, 0, 0)),
            ahead(3), ahead(4), ahead(5), ahead(6), ahead(7), ahead_nxt(3), ahead_nxt(4), ahead_nxt(5),
            pl.BlockSpec((None, 8, 6 * D_MODEL), lambda i: (l, 0, 0)),
            vec(D_MODEL), vec(D_MODEL),
            const((None, D_MODEL, N_BRANCH * D_MODEL), lambda i: (l, 0, 0)),
            const((None, N_BRANCH, BRANCH_W, D_MODEL), lambda i: (l, 0, 0, 0)),
            const((None, D_MODEL, D_MODEL), lambda i: (l, 0, 0)),
            const((None, D_MODEL, LANES), lambda i: (l, 0, 0)),
            const((None, D_MODEL, LANES), lambda i: (l, 0, 0)),
        ],
        out_specs=[rows(D_MODEL), rows(D_MODEL), pl.BlockSpec((N_EXPERTS, tm), lambda i: (0, i))],
        out_shape=[
            jax.ShapeDtypeStruct((N_TOK, D_MODEL), F32),
            jax.ShapeDtypeStruct((N_TOK, D_MODEL), BF16),
            jax.ShapeDtypeStruct((N_EXPERTS, N_TOK), F32),
        ],
        scratch_shapes=[
            pltpu.VMEM((CHUNK + 2 * HALO, W_CONV), F32),
            pltpu.VMEM((CHUNK + 2 * HALO, W_POOL), F32),
            pltpu.VMEM((SUBLANES, CHUNK + 2 * HALO - SUBLANES, W_CONV), F32),
            pltpu.VMEM((CHUNK, W_CONV), F32),
            pltpu.VMEM((CHUNK, W_SGU), F32),
            pltpu.VMEM((CHUNK, W_SGU), BF16),
            pltpu.VMEM((CHUNK, D_MODEL), BF16),
            pltpu.VMEM((CHUNK, D_MODEL), F32),
            pltpu.VMEM((CHUNK, D_MODEL), BF16),
            pltpu.VMEM((3, CHUNK, BRANCH_W), BF16),
            pltpu.VMEM((3, CHUNK, BRANCH_W), BF16),
        ],
        compiler_params=_cparams(1, 56),
        name="mixmerge",
    )(x, att_ctx, att_lat, *([proj] * 11), conv_w, conv_b, conv_ln_g, conv_ln_b, pool_w_bf, pool_scale,
      sgu_ln_g, sgu_ln_b, sgu_w_bf, sgu_bt, *([proj] * 8), mod, norm1_g, norm2_g, w_gate_half, w_branch_bf,
      w_out_half, w_router_hi, w_router_lo)


def _route_body(afft_ref, tri_ref, pos_ref, gate_ref, *, n_seq, seq_len, cap):
    a = jnp.concatenate([afft_ref[:, s * seq_len:(s + 1) * seq_len] for s in range(n_seq)], axis=0)
    n_rows = n_seq * N_EXPERTS
    capf = float(cap)

    def count(mask):
        return jnp.sum(jnp.where(mask, 1.0, 0.0), axis=1, keepdims=True)

    def as_float(bits):
        return lax.bitcast_convert_type(bits, F32)

    thr = jnp.zeros((n_rows, 1), jnp.int32)
    for bit in range(30, -1, -1):
        cand = thr | (1 << bit)
        thr = jnp.where(count(a >= as_float(cand)) >= capf, cand, thr)
    above = a >= as_float(thr + 1)
    tie = jnp.logical_and(a >= as_float(thr), jnp.logical_not(above))
    need = capf - count(above)

    idx = lax.broadcasted_iota(jnp.int32, (n_rows, seq_len), 1)
    last = jnp.zeros((n_rows, 1), jnp.int32)
    for bit in range(seq_len.bit_length() - 2, -1, -1):
        cand = last | (1 << bit)
        before = jnp.sum(jnp.where(tie, jnp.where(idx < cand, 1.0, 0.0), 0.0), axis=1, keepdims=True)
        last = jnp.where(before < need, cand, last)
    keep = jnp.where(above, 1.0, jnp.where(tie, jnp.where(idx <= last, 1.0, 0.0), 0.0))

    blk = tri_ref.shape[0]
    offset = jnp.zeros((n_rows, 1), F32)
    pos = []
    for c in range(seq_len // blk):
        kb = keep[:, c * blk:(c + 1) * blk]
        pos.append(_dot(kb.astype(BF16), tri_ref[...]) + offset)
        offset = offset + jnp.sum(kb, axis=1, keepdims=True)
    pos = jnp.concatenate(pos, axis=1)
    pos_ref[...] = jnp.where(keep > 0.0, pos, -1.0)
    gate_ref[...] = a


def _route(aff_t, tri, n_seq, seq_len, cap, col0):
    n_rows = n_seq * N_EXPERTS
    width = n_seq * seq_len
    shape = jax.ShapeDtypeStruct((n_rows, seq_len), F32)
    spec = pl.BlockSpec((n_rows, seq_len), lambda i: (0, 0))
    return pl.pallas_call(
        functools.partial(_route_body, n_seq=n_seq, seq_len=seq_len, cap=cap),
        grid=(1,),
        in_specs=[
            pl.BlockSpec((N_EXPERTS, width), lambda i: (0, col0 // width)),
            pl.BlockSpec(tri.shape, lambda i: (0, 0)),
        ],
        out_specs=[spec, spec],
        out_shape=[shape, shape],
        compiler_params=_cparams(1, 32),
        name="route_ctx" if cap == CAP_CTX else "route_lat",
    )(aff_t, tri)


def _slot_onehot(pos_ref, cap, values_ref=None, seq=0):
    n_tok = pos_ref.shape[1]
    slot = lax.broadcasted_iota(jnp.int32, (cap, n_tok), 0).astype(F32)
    rows = []
    for e in range(seq * N_EXPERTS, (seq + 1) * N_EXPERTS):
        hit = pos_ref[e:e + 1, :] == slot
        val = 1.0 if values_ref is None else values_ref[e:e + 1, :]
        rows.append(jnp.where(hit, val, 0.0).astype(BF16))
    return jnp.concatenate(rows, axis=0)


def _gather_ctx_body(pos_ref, h_ref, o_ref):
    for s in range(MOE_SEQS):
        onehot = _slot_onehot(pos_ref, CAP_CTX, seq=s)
        picked = _dot(onehot, h_ref[s * SEQ:(s + 1) * SEQ, :]).astype(BF16)
        o_ref[:, s * CAP_CTX:(s + 1) * CAP_CTX, :] = picked.reshape(N_EXPERTS, CAP_CTX, D_MODEL)


def _gather_ctx(pos, h2):
    return pl.pallas_call(
        _gather_ctx_body,
        grid=(BATCH // MOE_SEQS,),
        in_specs=[
            pl.BlockSpec((MOE_SEQS * N_EXPERTS, SEQ), lambda s: (s, 0)),
            pl.BlockSpec((MOE_SEQS * SEQ, D_MODEL), lambda s: (s, 0)),
        ],
        out_specs=pl.BlockSpec((N_EXPERTS, MOE_SEQS * CAP_CTX, D_MODEL), lambda s: (0, s, 0)),
        out_shape=jax.ShapeDtypeStruct((N_EXPERTS, BATCH * CAP_CTX, D_MODEL), BF16),
        compiler_params=_cparams(1, 32),
        name="gather_ctx",
    )(pos, h2)


def _gather_lat_body(pos_ref, h_ref, o_ref):
    e0 = pl.program_id(1) * LAT_EXPERTS
    slot = lax.broadcasted_iota(jnp.int32, (CAP_LAT, DEC_SEQ), 0).astype(F32)
    onehot = jnp.concatenate(
        [jnp.where(pos_ref[pl.ds(e0 + n, 1), :] == slot, 1.0, 0.0).astype(BF16) for n in range(LAT_EXPERTS)],
        axis=0)
    picked = _dot(onehot, h_ref[...]).astype(BF16)
    o_ref[...] = picked.reshape(LAT_EXPERTS, CAP_LAT, D_MODEL)


def _gather_lat(pos, h2):
    blk0 = N_CTX // DEC_SEQ
    return pl.pallas_call(
        _gather_lat_body,
        grid=(DEC_BATCH, N_EXPERTS // LAT_EXPERTS),
        in_specs=[
            pl.BlockSpec((N_EXPERTS, DEC_SEQ), lambda b, e: (b, 0)),
            pl.BlockSpec((DEC_SEQ, D_MODEL), lambda b, e: (blk0 + b, 0)),
        ],
        out_specs=pl.BlockSpec((LAT_EXPERTS, CAP_LAT, D_MODEL), lambda b, e: (e, b, 0)),
        out_shape=jax.ShapeDtypeStruct((N_EXPERTS, DEC_BATCH * CAP_LAT, D_MODEL), BF16),
        compiler_params=_cparams(2, 40),
        name="gather_lat",
    )(pos, h2)


def _ffn_body(xc_ref, xl_ref, w1_ref, w3_ref, w2_ref, yc_ref, yl_ref):
    w1 = w1_ref[...].astype(BF16)
    w3 = w3_ref[...].astype(BF16)
    w2 = w2_ref[...].astype(BF16)
    rows = 512
    for x_ref, y_ref in ((xc_ref, yc_ref), (xl_ref, yl_ref)):
        for r0 in range(0, x_ref.shape[0], rows):
            x = x_ref[r0:r0 + rows, :]
            a = _dot(x, w1)
            act = (a * _sigmoid(a) * _dot(x, w3)).astype(BF16)
            y_ref[r0:r0 + rows, :] = _dot(act, w2).astype(BF16)


def _ffn(xg_ctx, xg_lat, w1, w3, w2, l):
    n_c, n_l = xg_ctx.shape[1], xg_lat.shape[1]

    def wspec():
        return pl.BlockSpec((None, None, D_MODEL, D_MODEL), lambda e: (l, e, 0, 0))

    def xspec(n):
        return pl.BlockSpec((None, n, D_MODEL), lambda e: (e, 0, 0))

    return pl.pallas_call(
        _ffn_body,
        grid=(N_EXPERTS,),
        in_specs=[xspec(n_c), xspec(n_l), wspec(), wspec(), wspec()],
        out_specs=[xspec(n_c), xspec(n_l)],
        out_shape=[jax.ShapeDtypeStruct(xg_ctx.shape, BF16), jax.ShapeDtypeStruct(xg_lat.shape, BF16)],
        compiler_params=_cparams(1, 56),
        name="ffn",
    )(xg_ctx, xg_lat, w1, w3, w2)


def _combine_body(pos_ref, gate_ref, y_ref, x_ref, mod_ref, *rest, cap, seg0, n_seq):
    o_ref = rest[-1]
    seg = seg0 + pl.program_id(0) if seg0 else 0
    n_tok = pos_ref.shape[1]
    for s in range(n_seq):
        rows = slice(s * n_tok, (s + 1) * n_tok)
        weights = _slot_onehot(pos_ref, cap, gate_ref, seq=s)
        y = y_ref[:, s * cap:(s + 1) * cap, :].reshape(N_EXPERTS * cap, D_MODEL)
        out = x_ref[rows, :] + _mod_row(mod_ref, seg, 5) * _dot_tn(weights, y)
        if len(rest) == 2:
            out = out * lax.rsqrt(jnp.mean(out * out, axis=-1, keepdims=True) + EPS) * rest[0][...]
        o_ref[rows, :] = out


def _combine_ctx(pos, gate, y, x_mid, mod, l, final_g=None):
    last = final_g is not None
    return pl.pallas_call(
        functools.partial(_combine_body, cap=CAP_CTX, seg0=0, n_seq=MOE_SEQS),
        grid=(BATCH // MOE_SEQS,),
        in_specs=[
            pl.BlockSpec((MOE_SEQS * N_EXPERTS, SEQ), lambda s: (s, 0)),
            pl.BlockSpec((MOE_SEQS * N_EXPERTS, SEQ), lambda s: (s, 0)),
            pl.BlockSpec((N_EXPERTS, MOE_SEQS * CAP_CTX, D_MODEL), lambda s: (0, s, 0)),
            pl.BlockSpec((MOE_SEQS * SEQ, D_MODEL), lambda s: (s, 0)),
            pl.BlockSpec((None, 8, 6 * D_MODEL), lambda s: (l, 0, 0)),
        ] + ([pl.BlockSpec((1, D_MODEL), lambda s: (0, 0))] if last else []),
        out_specs=pl.BlockSpec((MOE_SEQS * SEQ, D_MODEL), lambda s: (s, 0)),
        out_shape=jax.ShapeDtypeStruct((N_CTX if last else N_TOK, D_MODEL), F32),
        input_output_aliases={} if last else {3: 0},
        compiler_params=_cparams(1, 32),
        name="combine_ctx",
    )(pos, gate, y, x_mid, mod, *([final_g] if last else []))


def _combine_lat(pos, gate, y, x_mid, mod, l, final_g=None):
    last = final_g is not None
    tiles = DEC_SEQ // LAT_TOKENS
    blk0 = N_CTX // LAT_TOKENS
    return pl.pallas_call(
        functools.partial(_combine_body, cap=CAP_LAT, seg0=1, n_seq=1),
        grid=(DEC_BATCH, tiles),
        in_specs=[
            pl.BlockSpec((N_EXPERTS, LAT_TOKENS), lambda b, t: (b, t)),
            pl.BlockSpec((N_EXPERTS, LAT_TOKENS), lambda b, t: (b, t)),
            pl.BlockSpec((N_EXPERTS, CAP_LAT, D_MODEL), lambda b, t: (0, b, 0)),
            pl.BlockSpec((LAT_TOKENS, D_MODEL), lambda b, t: (blk0 + b * tiles + t, 0)),
            pl.BlockSpec((None, 8, 6 * D_MODEL), lambda b, t: (l, 0, 0)),
        ] + ([pl.BlockSpec((1, D_MODEL), lambda b, t: (0, 0))] if last else []),
        out_specs=pl.BlockSpec((LAT_TOKENS, D_MODEL), lambda b, t: ((0 if last else blk0) + b * tiles + t, 0)),
        out_shape=jax.ShapeDtypeStruct((N_LAT if last else N_TOK, D_MODEL), F32),
        input_output_aliases={} if last else {3: 0},
        compiler_params=_cparams(2, 48),
        name="combine_lat",
    )(pos, gate, y, x_mid, mod, *([final_g] if last else []))


def kernel(x_prompt, x_sample, cache_k, cache_v, c, c_ctx, w_ada, b_ada, norm1_g, w_in, rpb, conv_w, conv_b,
           conv_ln_g, conv_ln_b, pool_w, pool_scale, sgu_ln_g, sgu_ln_b, sgu_w, sgu_b, w_branch, w_out,
           norm2_g, w_router, w1, w3, w2, final_g):
    def vecs(a):
        return a.reshape(DEPTH, 1, a.shape[-1])

    cond8 = jnp.concatenate([c_ctx[None, :], c, jnp.zeros((8 - 1 - DEC_BATCH, D_MODEL), F32)], axis=0)
    mod = _adaln(cond8, w_ada, b_ada)
    table = _rpb_table(rpb)
    cos_np, sin_np = _rope_tables()
    cos, sin = jnp.asarray(cos_np), jnp.asarray(sin_np)
    tri = jnp.asarray(np.triu(np.ones((CHUNK, CHUNK), np.float32), 1), BF16)

    w_gate_half = (0.5 * w_in[:, :, D_PROJ:]).astype(BF16)
    w_proj_bf = w_in[:, :, :D_PROJ].astype(BF16)
    w_branch_bf = w_branch.astype(BF16)
    w_out_half = (0.5 * w_out).astype(BF16)
    pool_w_bf = pool_w.astype(BF16)
    sgu_w_bf = sgu_w.astype(BF16)
    sgu_bt = sgu_b.transpose(0, 2, 1)
    w_router_pad = jnp.pad(w_router, ((0, 0), (0, 0), (0, LANES - N_EXPERTS)))
    w_router_hi = w_router_pad.astype(BF16)
    w_router_lo = (w_router_pad - w_router_hi.astype(F32)).astype(BF16)
    norm1_v, norm2_v = vecs(norm1_g), vecs(norm2_g)
    conv_b_v, conv_ln_g_v, conv_ln_b_v = vecs(conv_b), vecs(conv_ln_g), vecs(conv_ln_b)
    pool_scale_v, sgu_ln_g_v, sgu_ln_b_v = vecs(pool_scale), vecs(sgu_ln_g), vecs(sgu_ln_b)

    x = jnp.concatenate([x_prompt.reshape(N_CTX, D_MODEL), x_sample.reshape(N_LAT, D_MODEL)], axis=0)
    projs = []
    for l in range(DEPTH):
        proj = _proj(x, norm1_v, mod, w_proj_bf, l)
        projs.append(proj)
        att_ctx = _attn_ctx(proj)
        att_lat = _attn_lat(proj, cache_k, cache_v, table, cos, sin, l)
        x_mid, h2, aff_t = _mixmerge(x, att_ctx, att_lat, proj, mod, norm1_v, norm2_v, conv_w, conv_b_v,
                                     conv_ln_g_v, conv_ln_b_v, pool_w_bf, pool_scale_v, sgu_ln_g_v, sgu_ln_b_v,
                                     sgu_w_bf, sgu_bt, w_gate_half, w_branch_bf, w_out_half, w_router_hi,
                                     w_router_lo, l)
        pos_c, gate_c = _route(aff_t, tri, BATCH, SEQ, CAP_CTX, 0)
        pos_l, gate_l = _route(aff_t, tri, DEC_BATCH, DEC_SEQ, CAP_LAT, N_CTX)
        y_c, y_l = _ffn(_gather_ctx(pos_c, h2), _gather_lat(pos_l, h2), w1, w3, w2, l)
        if l < DEPTH - 1:
            x = _combine_ctx(pos_c, gate_c, y_c, x_mid, mod, l)
            x = _combine_lat(pos_l, gate_l, y_l, x, mod, l)
        else:
            y_prompt = _combine_ctx(pos_c, gate_c, y_c, x_mid, mod, l, final_g[None, :])
            y_sample = _combine_lat(pos_l, gate_l, y_l, x_mid, mod, l, final_g[None, :])

    y_prompt = y_prompt.reshape(BATCH, SEQ, D_MODEL)
    y_sample = y_sample.reshape(DEC_BATCH, DEC_SEQ, D_MODEL)
    new_k, new_v = _kv_out(projs)
    return (y_prompt, y_sample, new_k, new_v)
```
